```python
import jax, jax.numpy as jnp
from jax import lax
import numpy as np

D_MODEL = 1024
BATCH = 2
SEQ = 8192
DEPTH = 1

RET_HEADS = 4
RET_DK = 128
RET_DV = 128
RET_WIDTH = RET_HEADS * RET_DV
RET_CHUNK = 128

ATT_HEADS = 8
ATT_KV_HEADS = 2
ATT_HEAD_DIM = 64
ATT_WIDTH = ATT_HEADS * ATT_HEAD_DIM
ATT_KV_WIDTH = ATT_KV_HEADS * ATT_HEAD_DIM
WINDOW = 128
ATT_BLOCK = 128

D_MIX = RET_WIDTH + ATT_WIDTH
SPLIT_SIZES = (RET_HEADS * RET_DK, RET_HEADS * RET_DK, RET_WIDTH, RET_WIDTH,
               ATT_WIDTH, ATT_KV_WIDTH, ATT_KV_WIDTH, ATT_WIDTH)
IN_COLS = sum(SPLIT_SIZES)
EPS = 1e-6
NEG_BIG = -1e30

kernel_name = "hymba_retention_swa_sink_sandwich"


def rms_norm(x, g):
    xf = x.astype(jnp.float32)
    y = xf * lax.rsqrt(jnp.mean(xf * xf, axis=-1, keepdims=True) + EPS)
    return (y * g.astype(jnp.float32)).astype(x.dtype)


def retention(q, k, v):
    bsz, seq, nh, dk = q.shape
    dv = v.shape[-1]
    c = RET_CHUNK
    nc = seq // c
    f32 = jnp.float32
    log_gamma = jnp.log1p(-jnp.exp2(-5.0 - jnp.arange(nh, dtype=f32)))
    pos = jnp.arange(c, dtype=f32)
    diff = pos[:, None] - pos[None, :]
    intra_decay = jnp.where(diff >= 0,
                            jnp.exp(log_gamma[:, None, None] * jnp.maximum(diff, 0.0)),
                            0.0)
    xi = jnp.exp(log_gamma[:, None] * (pos + 1.0))
    zeta = jnp.exp(log_gamma[:, None] * (c - 1.0 - pos))
    chunk_decay = jnp.exp(log_gamma * c)

    qc = q.astype(f32).reshape(bsz, nc, c, nh, dk)
    kc = k.astype(f32).reshape(bsz, nc, c, nh, dk) * (dk ** -0.5)
    vc = v.astype(f32).reshape(bsz, nc, c, nh, dv)

    scores = jnp.einsum('bnihd,bnjhd->bnhij', qc, kc) * intra_decay
    inner = jnp.einsum('bnhij,bnjhe->bnihe', scores, vc)
    kv = jnp.einsum('bnjhd,hj,bnjhe->bnhde', kc, zeta, vc)

    def step(state, kv_c):
        return chunk_decay[None, :, None, None] * state + kv_c, state

    _, prev = lax.scan(step, jnp.zeros((bsz, nh, dk, dv), f32), jnp.moveaxis(kv, 1, 0))
    prev = jnp.moveaxis(prev, 0, 1)
    cross = jnp.einsum('bnihd,hi,bnhde->bnihe', qc, xi, prev)
    o = (inner + cross).reshape(bsz, seq, nh, dv)
    mu = jnp.mean(o, axis=-1, keepdims=True)
    var = jnp.mean(jnp.square(o - mu), axis=-1, keepdims=True)
    o = (o - mu) * lax.rsqrt(var + EPS)
    return o.reshape(bsz, seq, nh * dv).astype(q.dtype)


def swa_sink_attention(q, k, v, sinks):
    bsz, seq, hq, d = q.shape
    hkv = k.shape[2]
    grp = hq // hkv
    t = ATT_BLOCK
    nb = seq // t
    f32 = jnp.float32
    qb = q.reshape(bsz, nb, t, hkv, grp, d)
    kb = k.reshape(bsz, nb, t, hkv, d)
    vb = v.reshape(bsz, nb, t, hkv, d)
    pad = ((0, 0), (1, 0), (0, 0), (0, 0), (0, 0))
    kk = jnp.concatenate([jnp.pad(kb, pad)[:, :-1], kb], axis=2)
    vv = jnp.concatenate([jnp.pad(vb, pad)[:, :-1], vb], axis=2)

    s = jnp.einsum('bnqhgd,bnkhd->bnhgqk', qb, kk).astype(f32) * (d ** -0.5)
    qpos = jnp.arange(t) + t
    kpos = jnp.arange(2 * t)
    dist = qpos[:, None] - kpos[None, :]
    blk = jnp.arange(nb)
    valid = ((dist >= 0) & (dist < WINDOW))[None] & \
        ((blk[:, None, None] > 0) | (kpos[None, None, :] >= t))
    slopes = jnp.exp2(-8.0 * (jnp.arange(hq, dtype=f32) + 1.0) / hq).reshape(hkv, grp)
    s = s - slopes[:, :, None, None] * dist.astype(f32)
    s = jnp.where(valid[None, :, None, None], s, NEG_BIG)
    sink = jnp.broadcast_to(sinks.astype(f32).reshape(1, 1, hkv, grp, 1, 1),
                            s.shape[:-1] + (1,))
    p = jax.nn.softmax(jnp.concatenate([s, sink], axis=-1), axis=-1)[..., :-1]
    o = jnp.einsum('bnhgqk,bnkhd->bnqhgd', p.astype(v.dtype), vv)
    return o.reshape(bsz, seq, hq * d)


def setup_inputs(seed: int = 0) -> dict:
    key = jax.random.key(seed)
    ks = jax.random.split(key, 7)
    x = jax.random.normal(ks[0], (BATCH, SEQ, D_MODEL), jnp.float32)
    g_pre = 1.0 + 0.02 * jax.random.normal(ks[1], (D_MODEL,), jnp.float32)
    w_in = jax.random.normal(ks[2], (D_MODEL, IN_COLS), jnp.float32) * D_MODEL ** -0.5
    sinks = 0.5 * jax.random.normal(ks[3], (ATT_HEADS,), jnp.float32)
    w_out = jax.random.normal(ks[4], (D_MIX, D_MODEL), jnp.float32) * D_MIX ** -0.5
    g_post = 1.0 + 0.02 * jax.random.normal(ks[5], (D_MODEL,), jnp.float32)
    return {"x": x, "g_pre": g_pre, "w_in": w_in, "sinks": sinks,
            "w_out": w_out, "g_post": g_post}


def reference(x, g_pre, w_in, sinks, w_out, g_post):
    bsz, seq, _ = x.shape
    for _layer in range(DEPTH):
        h = rms_norm(x, g_pre)
        proj = jnp.einsum('bsd,dc->bsc', h, w_in)
        bounds = list(np.cumsum(SPLIT_SIZES)[:-1])
        rq, rk, rv, rg, aq, ak, av, ag = jnp.split(proj, bounds, axis=-1)
        ret = retention(rq.reshape(bsz, seq, RET_HEADS, RET_DK),
                        rk.reshape(bsz, seq, RET_HEADS, RET_DK),
                        rv.reshape(bsz, seq, RET_HEADS, RET_DV))
        ret = ret * jax.nn.silu(rg)
        att = swa_sink_attention(aq.reshape(bsz, seq, ATT_HEADS, ATT_HEAD_DIM),
                                 ak.reshape(bsz, seq, ATT_KV_HEADS, ATT_HEAD_DIM),
                                 av.reshape(bsz, seq, ATT_KV_HEADS, ATT_HEAD_DIM),
                                 sinks)
        att = att * jax.nn.silu(ag)
        mixed = jnp.concatenate([ret, att], axis=-1)
        out = jnp.einsum('bsc,cd->bsd', mixed, w_out)
        x = x + rms_norm(out, g_post)
    return x
```

```python
import functools

import jax
import jax.numpy as jnp
from jax import lax
from jax.experimental import pallas as pl
from jax.experimental.pallas import tpu as pltpu

D_MODEL = 1024
RET_HEADS = 4
RET_DK = 128
RET_DV = 128
RET_WIDTH = RET_HEADS * RET_DV
CHUNK = 128
ATT_HEADS = 8
ATT_KV_HEADS = 2
ATT_GROUP = ATT_HEADS // ATT_KV_HEADS
ATT_HEAD_DIM = 64
ATT_WIDTH = ATT_HEADS * ATT_HEAD_DIM
ATT_KV_WIDTH = ATT_KV_HEADS * ATT_HEAD_DIM
D_MIX = RET_WIDTH + ATT_WIDTH
EPS = 1e-6
NEG_BIG = -1e30

COL_RQ = 0
COL_RK = COL_RQ + RET_HEADS * RET_DK
COL_RV = COL_RK + RET_HEADS * RET_DK
COL_RG = COL_RV + RET_WIDTH
COL_AQ = COL_RG + RET_WIDTH
COL_AK = COL_AQ + ATT_WIDTH
COL_AV = COL_AK + ATT_KV_WIDTH
COL_AG = COL_AV + ATT_KV_WIDTH
IN_COLS = COL_AG + ATT_WIDTH

TM = 512
VMEM_LIMIT_BYTES = 56 * 1024 * 1024

_NT = (((1,), (1,)), ((), ()))


def _silu(x):
    return x * (1.0 / (1.0 + jnp.exp(-x)))


def _layer_kernel(sinks_ref, cdecay_ref, x_ref, gpre_ref, win_ref, wout_ref,
                  gpost_ref, decay_ref, xi_ref, zeta_ref, bias_ref,
                  o_ref, proj_ref, state_ref, mixed_ref, *, tm):
    f32, bf16 = jnp.float32, jnp.bfloat16
    t = pl.program_id(1)

    @pl.when(t == 0)
    def _():
        state_ref[...] = jnp.zeros_like(state_ref)
        proj_ref[0:CHUNK, COL_AK:COL_AG] = jnp.zeros((CHUNK, COL_AG - COL_AK), f32)

    x = x_ref[0]
    ms = jnp.mean(x * x, axis=-1, keepdims=True)
    h = (x * lax.rsqrt(ms + EPS) * gpre_ref[...]).astype(bf16)
    proj_ref[CHUNK:CHUNK + tm, :] = jnp.dot(h, win_ref[...], preferred_element_type=f32)

    qi = lax.broadcasted_iota(jnp.int32, (CHUNK, 2 * CHUNK), 0)
    kj = lax.broadcasted_iota(jnp.int32, (CHUNK, 2 * CHUNK), 1)
    dist = qi + CHUNK - kj
    band = (dist >= 0) & (dist < CHUNK)

    def chunk_body(c, carry):
        r0 = pl.multiple_of(c * CHUNK, CHUNK)
        rows = pl.ds(r0 + CHUNK, CHUNK)
        win = pl.ds(r0, 2 * CHUNK)
        mrows = pl.ds(r0, CHUNK)

        for hd in range(RET_HEADS):
            q = proj_ref[rows, COL_RQ + hd * RET_DK:COL_RQ + (hd + 1) * RET_DK]
            k = proj_ref[rows, COL_RK + hd * RET_DK:COL_RK + (hd + 1) * RET_DK]
            v = proj_ref[rows, COL_RV + hd * RET_DV:COL_RV + (hd + 1) * RET_DV]
            g = proj_ref[rows, COL_RG + hd * RET_DV:COL_RG + (hd + 1) * RET_DV]
            vb = v.astype(bf16)
            sc = lax.dot_general(q.astype(bf16), k.astype(bf16), _NT,
                                 preferred_element_type=f32) * decay_ref[hd]
            inner = jnp.dot(sc.astype(bf16), vb, preferred_element_type=f32)
            st = state_ref[hd]
            cross = jnp.dot((q * xi_ref[hd]).astype(bf16), st.astype(bf16),
                            preferred_element_type=f32)
            kzt = (k * zeta_ref[hd]).T.astype(bf16)
            kv = jnp.dot(kzt, vb, preferred_element_type=f32)
            state_ref[hd] = cdecay_ref[hd] * st + kv
            o = inner + cross
            mu = jnp.mean(o, axis=-1, keepdims=True)
            d = o - mu
            var = jnp.mean(d * d, axis=-1, keepdims=True)
            on = d * lax.rsqrt(var + EPS)
            mixed_ref[mrows, hd * RET_DV:(hd + 1) * RET_DV] = (on * _silu(g)).astype(bf16)

        first = jnp.logical_and(t == 0, c == 0)
        valid = band & ((kj >= CHUNK) | jnp.logical_not(first))
        for kvh in range(ATT_KV_HEADS):
            kw = proj_ref[win, COL_AK + kvh * ATT_HEAD_DIM:COL_AK + (kvh + 1) * ATT_HEAD_DIM]
            vw = proj_ref[win, COL_AV + kvh * ATT_HEAD_DIM:COL_AV + (kvh + 1) * ATT_HEAD_DIM]
            kb = kw.astype(bf16)
            vb = vw.astype(bf16)
            for j in range(ATT_GROUP):
                hq = kvh * ATT_GROUP + j
                q = proj_ref[rows, COL_AQ + hq * ATT_HEAD_DIM:COL_AQ + (hq + 1) * ATT_HEAD_DIM]
                g = proj_ref[rows, COL_AG + hq * ATT_HEAD_DIM:COL_AG + (hq + 1) * ATT_HEAD_DIM]
                s = lax.dot_general((q * (ATT_HEAD_DIM ** -0.5)).astype(bf16), kb, _NT,
                                    preferred_element_type=f32) - bias_ref[hq]
                s = jnp.where(valid, s, NEG_BIG)
                sink = sinks_ref[hq]
                m = jnp.maximum(jnp.max(s, axis=-1, keepdims=True), sink)
                e = jnp.exp(s - m)
                den = jnp.sum(e, axis=-1, keepdims=True) + jnp.exp(sink - m)
                o = jnp.dot(e.astype(bf16), vb, preferred_element_type=f32) * (1.0 / den)
                mixed_ref[mrows, RET_WIDTH + hq * ATT_HEAD_DIM:RET_WIDTH + (hq + 1) * ATT_HEAD_DIM] = (
                    o * _silu(g)).astype(bf16)
        return carry

    lax.fori_loop(0, tm // CHUNK, chunk_body, 0)

    proj_ref[0:CHUNK, COL_AK:COL_AG] = proj_ref[tm:tm + CHUNK, COL_AK:COL_AG]

    out = jnp.dot(mixed_ref[...], wout_ref[...], preferred_element_type=f32)
    ms2 = jnp.mean(out * out, axis=-1, keepdims=True)
    o_ref[0] = x_ref[0] + out * lax.rsqrt(ms2 + EPS) * gpost_ref[...]


def _retention_tables():
    f32 = jnp.float32
    c = CHUNK
    log_gamma = jnp.log1p(-jnp.exp2(-5.0 - jnp.arange(RET_HEADS, dtype=f32)))
    pos = jnp.arange(c, dtype=f32)
    diff = pos[:, None] - pos[None, :]
    intra = jnp.where(diff >= 0,
                      jnp.exp(log_gamma[:, None, None] * jnp.maximum(diff, 0.0)), 0.0)
    decay = intra * (RET_DK ** -0.5)
    xi = jnp.exp(log_gamma[:, None] * (pos + 1.0))
    zeta = jnp.exp(log_gamma[:, None] * (c - 1.0 - pos)) * (RET_DK ** -0.5)
    xi_b = jnp.broadcast_to(xi[:, :, None], (RET_HEADS, c, RET_DK))
    zeta_b = jnp.broadcast_to(zeta[:, :, None], (RET_HEADS, c, RET_DK))
    chunk_decay = jnp.exp(log_gamma * c)
    return decay, xi_b, zeta_b, chunk_decay


def _alibi_bias():
    f32 = jnp.float32
    t = CHUNK
    qpos = jnp.arange(t) + t
    kpos = jnp.arange(2 * t)
    dist = (qpos[:, None] - kpos[None, :]).astype(f32)
    slopes = jnp.exp2(-8.0 * (jnp.arange(ATT_HEADS, dtype=f32) + 1.0) / ATT_HEADS)
    return slopes[:, None, None] * dist[None]


def kernel(x, g_pre, w_in, sinks, w_out, g_post):
    bsz, seq, d_model = x.shape
    assert d_model == D_MODEL and w_in.shape == (D_MODEL, IN_COLS)
    assert w_out.shape == (D_MIX, D_MODEL) and seq % TM == 0
    f32, bf16 = jnp.float32, jnp.bfloat16
    decay, xi_b, zeta_b, chunk_decay = _retention_tables()
    bias = _alibi_bias()

    def const(shape):
        return pl.BlockSpec(shape, lambda b, t: (0,) * len(shape))

    smem = pl.BlockSpec(memory_space=pltpu.SMEM)
    return pl.pallas_call(
        functools.partial(_layer_kernel, tm=TM),
        grid=(bsz, seq // TM),
        in_specs=[
            smem,
            smem,
            pl.BlockSpec((1, TM, D_MODEL), lambda b, t: (b, t, 0)),
            const((1, D_MODEL)),
            const((D_MODEL, IN_COLS)),
            const((D_MIX, D_MODEL)),
            const((1, D_MODEL)),
            const((RET_HEADS, CHUNK, CHUNK)),
            const((RET_HEADS, CHUNK, RET_DK)),
            const((RET_HEADS, CHUNK, RET_DK)),
            const((ATT_HEADS, CHUNK, 2 * CHUNK)),
        ],
        out_specs=pl.BlockSpec((1, TM, D_MODEL), lambda b, t: (b, t, 0)),
        out_shape=jax.ShapeDtypeStruct(x.shape, x.dtype),
        scratch_shapes=[
            pltpu.VMEM((TM + CHUNK, IN_COLS), f32),
            pltpu.VMEM((RET_HEADS, RET_DK, RET_DV), f32),
            pltpu.VMEM((TM, D_MIX), bf16),
        ],
        compiler_params=pltpu.CompilerParams(
            dimension_semantics=("arbitrary", "arbitrary"),
            vmem_limit_bytes=VMEM_LIMIT_BYTES),
        name="hybrid_layer",
    )(sinks.astype(f32), chunk_decay, x, g_pre.reshape(1, D_MODEL).astype(f32),
      w_in.astype(bf16), w_out.astype(bf16), g_post.reshape(1, D_MODEL).astype(f32),
      decay, xi_b, zeta_b, bias)
```

```python
import functools

import jax
import jax.numpy as jnp
from jax import lax
from jax.experimental import pallas as pl
from jax.experimental.pallas import tpu as pltpu

D_MODEL = 1024
RET_HEADS = 4
RET_DK = 128
RET_DV = 128
RET_WIDTH = RET_HEADS * RET_DV
CHUNK = 128
ATT_HEADS = 8
ATT_KV_HEADS = 2
ATT_GROUP = ATT_HEADS // ATT_KV_HEADS
ATT_HEAD_DIM = 64
ATT_WIDTH = ATT_HEADS * ATT_HEAD_DIM
ATT_KV_WIDTH = ATT_KV_HEADS * ATT_HEAD_DIM
D_MIX = RET_WIDTH + ATT_WIDTH
EPS = 1e-6
BIG = 1e30
LOG2E = 1.4426950408889634

COL_RQ = 0
COL_RK = COL_RQ + RET_HEADS * RET_DK
COL_RV = COL_RK + RET_HEADS * RET_DK
COL_RG = COL_RV + RET_WIDTH
COL_AQ = COL_RG + RET_WIDTH
COL_AK = COL_AQ + ATT_WIDTH
COL_AV = COL_AK + ATT_KV_WIDTH
COL_AG = COL_AV + ATT_KV_WIDTH
IN_COLS = COL_AG + ATT_WIDTH

PAIR = 2 * ATT_HEAD_DIM
assert PAIR == CHUNK and ATT_KV_WIDTH == CHUNK and RET_DK == CHUNK and RET_DV == CHUNK

TM = 512
VMEM_LIMIT_BYTES = 56 * 1024 * 1024
N_CARRY = 5


def _silu(x):
    hx = 0.5 * x
    return hx + hx * jnp.tanh(hx)


def _dot(a, b):
    return jnp.dot(a, b, preferred_element_type=jnp.float32)


def _retention_head(hd, q, k, v, gate, state_ref, cdecay, decay_ref, xi_ref, zeta_t_ref):
    bf16 = jnp.bfloat16
    k_t = k.T
    vb = v.astype(bf16)
    sc = _dot(q.astype(bf16), k_t.astype(bf16)) * decay_ref[hd]
    st = state_ref[hd]
    lhs = jnp.concatenate([sc.astype(bf16), (q * xi_ref[hd]).astype(bf16)], axis=1)
    rhs = jnp.concatenate([vb, st.astype(bf16)], axis=0)
    o = _dot(lhs, rhs)
    state_ref[hd] = cdecay * st + _dot((k_t * zeta_t_ref[hd]).astype(bf16), vb)
    mu = jnp.mean(o, axis=-1, keepdims=True)
    d = o - mu
    var = jnp.mean(d * d, axis=-1, keepdims=True)
    return d * lax.rsqrt(var + EPS) * _silu(gate)


def _attention_prep(k_both, v_both, lo):
    bf16 = jnp.bfloat16
    k_t = (k_both.T * (ATT_HEAD_DIM ** -0.5 * LOG2E)).astype(bf16)
    v_roll = pltpu.roll(v_both, ATT_HEAD_DIM, axis=1)
    return (k_t,
            jnp.where(lo, v_both, 0.0).astype(bf16),
            jnp.where(lo, 0.0, v_roll).astype(bf16),
            jnp.where(lo, v_roll, 0.0).astype(bf16),
            jnp.where(lo, 0.0, v_both).astype(bf16))


def _attention_group(g, q_g, cur, prev, bias_ref, first_idx, sinks_ref, tri, lo):
    f32, bf16 = jnp.float32, jnp.bfloat16
    hs = slice(g * ATT_HEAD_DIM, (g + 1) * ATT_HEAD_DIM)
    k_win = jnp.concatenate([cur[0][hs], prev[0][hs]], axis=1)
    zeros = jnp.zeros_like(k_win)
    rhs_even = jnp.concatenate([k_win, zeros], axis=0)
    rhs_odd = jnp.concatenate([zeros, k_win], axis=0)
    lhs = jnp.concatenate([q_g[:, :PAIR], q_g[:, PAIR:]], axis=0)
    s_par = (_dot(lhs, rhs_even), _dot(lhs, rhs_odd))

    p_rows, sink_terms = [], []
    for p in range(2):
        parts, sink_e = [], []
        for e in range(2):
            hq = g * ATT_GROUP + p * 2 + e
            s = s_par[e][p * CHUNK:(p + 1) * CHUNK]
            sf = jnp.where(tri, s[:, :CHUNK], s[:, CHUNK:]) - bias_ref[first_idx, hq]
            m = jnp.max(sf, axis=-1, keepdims=True)
            ex = jnp.exp2(sf - m)
            sink_e.append(jnp.exp2(sinks_ref[hq] - m))
            parts += [jnp.where(tri, ex, 0.0).astype(bf16), jnp.where(tri, 0.0, ex).astype(bf16)]
        p_rows.append(jnp.concatenate(parts, axis=1))
        sink_terms.append(jnp.where(lo, sink_e[0], sink_e[1]))
    ones_l = jnp.where(lo, 1.0, 0.0).astype(bf16)
    ones_r = jnp.where(lo, 0.0, 1.0).astype(bf16)
    v_l, v_r = cur[1 + 2 * g], cur[2 + 2 * g]
    pv_l, pv_r = prev[1 + 2 * g], prev[2 + 2 * g]
    rhs = jnp.concatenate([jnp.concatenate([v_l, ones_l], axis=1),
                           jnp.concatenate([pv_l, ones_l], axis=1),
                           jnp.concatenate([v_r, ones_r], axis=1),
                           jnp.concatenate([pv_r, ones_r], axis=1)], axis=0)
    out = _dot(jnp.concatenate(p_rows, axis=0), rhs)
    res = []
    for p in range(2):
        o = out[p * CHUNK:(p + 1) * CHUNK]
        res.append(o[:, :PAIR] / (o[:, PAIR:] + sink_terms[p]))
    return res


def _layer_kernel(sinks_ref, cdecay_ref, x_ref, gpre_ref, win_ref, wout_ref,
                  gpost_ref, decay_ref, xi_ref, zeta_t_ref, bias_ref,
                  o_ref, proj_ref, state_ref, carry_ref, mixed_ref, *, tm):
    f32, bf16 = jnp.float32, jnp.bfloat16
    t = pl.program_id(1)

    @pl.when(t == 0)
    def _():
        state_ref[...] = jnp.zeros_like(state_ref)
        carry_ref[...] = jnp.zeros_like(carry_ref)

    x = x_ref[0]
    ms = jnp.mean(x * x, axis=-1, keepdims=True)
    h = (x * lax.rsqrt(ms + EPS) * gpre_ref[...]).astype(bf16)
    proj_ref[...] = _dot(h, win_ref[...])

    row = lax.broadcasted_iota(jnp.int32, (CHUNK, CHUNK), 0)
    lane = lax.broadcasted_iota(jnp.int32, (CHUNK, CHUNK), 1)
    tri = lane <= row
    lo = lane < ATT_HEAD_DIM

    def chunk_body(c, carry):
        rows = pl.ds(pl.multiple_of(c * CHUNK, CHUNK), CHUNK)

        def cols(base, i, width=CHUNK):
            return proj_ref[rows, base + i * width:base + (i + 1) * width]

        for hd in range(RET_HEADS):
            mixed_ref[rows, hd * RET_DV:(hd + 1) * RET_DV] = _retention_head(
                hd, cols(COL_RQ, hd), cols(COL_RK, hd), cols(COL_RV, hd), cols(COL_RG, hd),
                state_ref, cdecay_ref[hd], decay_ref, xi_ref, zeta_t_ref).astype(bf16)

        first_idx = jnp.logical_and(t == 0, c == 0).astype(jnp.int32)
        prev = tuple(carry_ref[i] for i in range(N_CARRY))
        cur = _attention_prep(cols(COL_AK, 0), cols(COL_AV, 0), lo)
        for g in range(ATT_KV_HEADS):
            q_g = cols(COL_AQ, g, 2 * PAIR).astype(bf16)
            pairs = _attention_group(g, q_g, cur, prev, bias_ref, first_idx, sinks_ref, tri, lo)
            for p in range(2):
                i = g * 2 + p
                mixed_ref[rows, RET_WIDTH + i * PAIR:RET_WIDTH + (i + 1) * PAIR] = (
                    pairs[p] * _silu(cols(COL_AG, i))).astype(bf16)
        for i in range(N_CARRY):
            carry_ref[i] = cur[i]
        return carry

    lax.fori_loop(0, tm // CHUNK, chunk_body, 0)

    out = _dot(mixed_ref[...], wout_ref[...])
    ms2 = jnp.mean(out * out, axis=-1, keepdims=True)
    o_ref[0] = x_ref[0] + out * lax.rsqrt(ms2 + EPS) * gpost_ref[...]


def _retention_tables():
    f32 = jnp.float32
    c = CHUNK
    log_gamma = jnp.log1p(-jnp.exp2(-5.0 - jnp.arange(RET_HEADS, dtype=f32)))
    pos = jnp.arange(c, dtype=f32)
    diff = pos[:, None] - pos[None, :]
    intra = jnp.where(diff >= 0,
                      jnp.exp(log_gamma[:, None, None] * jnp.maximum(diff, 0.0)), 0.0)
    decay = intra * (RET_DK ** -0.5)
    xi = jnp.exp(log_gamma[:, None] * (pos + 1.0))
    zeta = jnp.exp(log_gamma[:, None] * (c - 1.0 - pos)) * (RET_DK ** -0.5)
    xi_b = jnp.broadcast_to(xi[:, :, None], (RET_HEADS, c, RET_DK))
    zeta_t = jnp.broadcast_to(zeta[:, None, :], (RET_HEADS, RET_DK, c))
    chunk_decay = jnp.exp(log_gamma * c)
    return decay, xi_b, zeta_t, chunk_decay


def _alibi_bias():
    f32 = jnp.float32
    t = CHUNK
    i = jnp.arange(t)[:, None]
    j = jnp.arange(t)[None, :]
    dist = jnp.where(j <= i, i - j, i + t - j).astype(f32)
    slopes = jnp.exp2(-8.0 * (jnp.arange(ATT_HEADS, dtype=f32) + 1.0) / ATT_HEADS)
    bias = LOG2E * slopes[:, None, None] * dist[None]
    first = jnp.where((j <= i)[None], bias, BIG)
    return jnp.stack([bias, first])


def kernel(x, g_pre, w_in, sinks, w_out, g_post):
    bsz, seq, d_model = x.shape
    assert d_model == D_MODEL and w_in.shape == (D_MODEL, IN_COLS)
    assert w_out.shape == (D_MIX, D_MODEL) and seq % TM == 0
    f32, bf16 = jnp.float32, jnp.bfloat16
    decay, xi_b, zeta_t, chunk_decay = _retention_tables()
    bias = _alibi_bias()

    def const(shape):
        return pl.BlockSpec(shape, lambda b, t: (0,) * len(shape))

    smem = pl.BlockSpec(memory_space=pltpu.SMEM)
    return pl.pallas_call(
        functools.partial(_layer_kernel, tm=TM),
        grid=(bsz, seq // TM),
        in_specs=[
            smem,
            smem,
            pl.BlockSpec((1, TM, D_MODEL), lambda b, t: (b, t, 0)),
            const((1, D_MODEL)),
            const((D_MODEL, IN_COLS)),
            const((D_MIX, D_MODEL)),
            const((1, D_MODEL)),
            const((RET_HEADS, CHUNK, CHUNK)),
            const((RET_HEADS, CHUNK, RET_DK)),
            const((RET_HEADS, RET_DK, CHUNK)),
            const((2, ATT_HEADS, CHUNK, CHUNK)),
        ],
        out_specs=pl.BlockSpec((1, TM, D_MODEL), lambda b, t: (b, t, 0)),
        out_shape=jax.ShapeDtypeStruct(x.shape, x.dtype),
        scratch_shapes=[
            pltpu.VMEM((TM, IN_COLS), f32),
            pltpu.VMEM((RET_HEADS, RET_DK, RET_DV), f32),
            pltpu.VMEM((N_CARRY, CHUNK, CHUNK), bf16),
            pltpu.VMEM((TM, D_MIX), bf16),
        ],
        compiler_params=pltpu.CompilerParams(
            dimension_semantics=("arbitrary", "arbitrary"),
            vmem_limit_bytes=VMEM_LIMIT_BYTES),
        name="hybrid_layer",
    )(sinks.astype(f32) * LOG2E, chunk_decay, x, g_pre.reshape(1, D_MODEL).astype(f32),
      w_in.astype(bf16), w_out.astype(bf16), g_post.reshape(1, D_MODEL).astype(f32),
      decay, xi_b, zeta_t, bias)
```

```python
import functools

import jax
import jax.numpy as jnp
from jax import lax
from jax.experimental import pallas as pl
from jax.experimental.pallas import tpu as pltpu

D_MODEL = 1024
RET_HEADS = 4
RET_DK = 128
RET_DV = 128
RET_WIDTH = RET_HEADS * RET_DV
CHUNK = 128
ATT_HEADS = 8
ATT_KV_HEADS = 2
ATT_GROUP = ATT_HEADS // ATT_KV_HEADS
ATT_HEAD_DIM = 64
ATT_WIDTH = ATT_HEADS * ATT_HEAD_DIM
ATT_KV_WIDTH = ATT_KV_HEADS * ATT_HEAD_DIM
D_MIX = RET_WIDTH + ATT_WIDTH
EPS = 1e-6
BIG = 1e30
LOG2E = 1.4426950408889634

COL_RQ = 0
COL_RK = COL_RQ + RET_HEADS * RET_DK
COL_RV = COL_RK + RET_HEADS * RET_DK
COL_RG = COL_RV + RET_WIDTH
COL_AQ = COL_RG + RET_WIDTH
COL_AK = COL_AQ + ATT_WIDTH
COL_AV = COL_AK + ATT_KV_WIDTH
COL_AG = COL_AV + ATT_KV_WIDTH
IN_COLS = COL_AG + ATT_WIDTH

PAIR = 2 * ATT_HEAD_DIM
assert PAIR == CHUNK and ATT_KV_WIDTH == CHUNK and RET_DK == CHUNK and RET_DV == CHUNK

TM = 256
VMEM_LIMIT_BYTES = 56 * 1024 * 1024
N_CARRY = 5
PROJ_COLS = 256
assert IN_COLS % PROJ_COLS == 0
_DONE = object()


def _silu(x):
    hx = 0.5 * x
    return hx + hx * jnp.tanh(hx)


def _dot(a, b):
    return jnp.dot(a, b, preferred_element_type=jnp.float32)


def _retention_scores(q, k):
    k_t = k.T
    return _dot(q.astype(jnp.bfloat16), k_t.astype(jnp.bfloat16)), k_t


def _retention_mix(hd, sc, q, k_t, v, state_ref, cdecay, decay_ref, xi_ref, zeta_t_ref):
    bf16 = jnp.bfloat16
    vb = v.astype(bf16)
    st = state_ref[hd]
    lhs = jnp.concatenate([(sc * decay_ref[hd]).astype(bf16), (q * xi_ref[hd]).astype(bf16)], axis=1)
    rhs = jnp.concatenate([vb, st.astype(bf16)], axis=0)
    o = _dot(lhs, rhs)
    state_ref[hd] = cdecay * st + _dot((k_t * zeta_t_ref[hd]).astype(bf16), vb)
    return o


def _retention_norm_gate(o, gate):
    mu = jnp.mean(o, axis=-1, keepdims=True)
    d = o - mu
    var = jnp.mean(d * d, axis=-1, keepdims=True)
    return d * lax.rsqrt(var + EPS) * _silu(gate)


def _attention_prep(k_both, v_both, lo):
    bf16 = jnp.bfloat16
    k_t = (k_both.T * (ATT_HEAD_DIM ** -0.5 * LOG2E)).astype(bf16)
    v_roll = pltpu.roll(v_both, ATT_HEAD_DIM, axis=1)
    return (k_t,
            jnp.where(lo, v_both, 0.0).astype(bf16),
            jnp.where(lo, 0.0, v_roll).astype(bf16),
            jnp.where(lo, v_roll, 0.0).astype(bf16),
            jnp.where(lo, 0.0, v_both).astype(bf16))


def _attention_scores(g, q_g, cur, prev):
    hs = slice(g * ATT_HEAD_DIM, (g + 1) * ATT_HEAD_DIM)
    k_win = jnp.concatenate([cur[0][hs], prev[0][hs]], axis=1)
    zeros = jnp.zeros_like(k_win)
    rhs_even = jnp.concatenate([k_win, zeros], axis=0)
    rhs_odd = jnp.concatenate([zeros, k_win], axis=0)
    lhs = jnp.concatenate([q_g[:, :PAIR], q_g[:, PAIR:]], axis=0)
    return _dot(lhs, rhs_even), _dot(lhs, rhs_odd)


def _attention_probs(g, s_par, bias_ref, first_idx, sinks_ref, tri, lo):
    bf16 = jnp.bfloat16
    p_rows, sink_terms = [], []
    for p in range(2):
        parts, sink_e = [], []
        for e in range(2):
            hq = g * ATT_GROUP + p * 2 + e
            s = s_par[e][p * CHUNK:(p + 1) * CHUNK]
            sf = jnp.where(tri, s[:, :CHUNK], s[:, CHUNK:]) - bias_ref[first_idx, hq]
            m = jnp.max(sf, axis=-1, keepdims=True)
            ex = jnp.exp2(sf - m)
            sink_e.append(jnp.exp2(sinks_ref[hq] - m))
            parts += [jnp.where(tri, ex, 0.0).astype(bf16), jnp.where(tri, 0.0, ex).astype(bf16)]
        p_rows.append(jnp.concatenate(parts, axis=1))
        sink_terms.append(jnp.where(lo, sink_e[0], sink_e[1]))
    return jnp.concatenate(p_rows, axis=0), sink_terms


def _attention_pv(g, probs, cur, prev, lo):
    bf16 = jnp.bfloat16
    ones_l = jnp.where(lo, 1.0, 0.0).astype(bf16)
    ones_r = jnp.where(lo, 0.0, 1.0).astype(bf16)
    v_l, v_r = cur[1 + 2 * g], cur[2 + 2 * g]
    pv_l, pv_r = prev[1 + 2 * g], prev[2 + 2 * g]
    rhs = jnp.concatenate([jnp.concatenate([v_l, ones_l], axis=1),
                           jnp.concatenate([pv_l, ones_l], axis=1),
                           jnp.concatenate([v_r, ones_r], axis=1),
                           jnp.concatenate([pv_r, ones_r], axis=1)], axis=0)
    return _dot(probs, rhs)


def _layer_kernel(sinks_ref, cdecay_ref, xn_ref, xp_ref, gpre_ref, win_ref, wout_ref,
                  gpost_ref, decay_ref, xi_ref, zeta_t_ref, bias_ref,
                  o_ref, proj_a, proj_b, state_ref, carry_ref, mixed_ref, h_ref, *, tm,
                  tiles_per_seq):
    f32, bf16 = jnp.float32, jnp.bfloat16
    n = pl.program_id(0)
    seq_start = (jnp.maximum(n - 1, 0) % tiles_per_seq) == 0

    @pl.when(n == 0)
    def _():
        proj_b[...] = jnp.zeros_like(proj_b)

    @pl.when(seq_start)
    def _():
        state_ref[...] = jnp.zeros_like(state_ref)
        carry_ref[...] = jnp.zeros_like(carry_ref)

    row = lax.broadcasted_iota(jnp.int32, (CHUNK, CHUNK), 0)
    lane = lax.broadcasted_iota(jnp.int32, (CHUNK, CHUNK), 1)
    tri = lane <= row
    lo = lane < ATT_HEAD_DIM

    def stage1(proj_w):
        x = xn_ref[...]
        ms = jnp.mean(x * x, axis=-1, keepdims=True)
        h_ref[...] = (x * lax.rsqrt(ms + EPS) * gpre_ref[...]).astype(bf16)
        yield
        for j in range(0, IN_COLS, PROJ_COLS):
            proj_w[:, j:j + PROJ_COLS] = _dot(h_ref[...], win_ref[:, j:j + PROJ_COLS])
            yield

    def stage2(proj_r):
        prev = tuple(carry_ref[i] for i in range(N_CARRY))
        for c in range(tm // CHUNK):
            rows = slice(c * CHUNK, (c + 1) * CHUNK)

            def cols(base, i, width=CHUNK):
                return proj_r[rows, base + i * width:base + (i + 1) * width]

            heads = range(RET_HEADS)
            scored = [_retention_scores(cols(COL_RQ, hd), cols(COL_RK, hd)) for hd in heads]
            yield
            ret = [_retention_mix(hd, scored[hd][0], cols(COL_RQ, hd), scored[hd][1],
                                  cols(COL_RV, hd), state_ref, cdecay_ref[hd], decay_ref,
                                  xi_ref, zeta_t_ref) for hd in heads]
            yield
            first_idx = seq_start.astype(jnp.int32) if c == 0 else 0
            cur = _attention_prep(cols(COL_AK, 0), cols(COL_AV, 0), lo)
            groups = range(ATT_KV_HEADS)
            s_par = [_attention_scores(g, cols(COL_AQ, g, 2 * PAIR).astype(bf16), cur, prev)
                     for g in groups]
            yield
            for hd in heads:
                mixed_ref[rows, hd * RET_DV:(hd + 1) * RET_DV] = _retention_norm_gate(
                    ret[hd], cols(COL_RG, hd)).astype(bf16)
            probs = [_attention_probs(g, s_par[g], bias_ref, first_idx, sinks_ref, tri, lo)
                     for g in groups]
            yield
            pv = [_attention_pv(g, probs[g][0], cur, prev, lo) for g in groups]
            yield
            for g in groups:
                for p in range(2):
                    i = g * 2 + p
                    o = pv[g][p * CHUNK:(p + 1) * CHUNK]
                    att = o[:, :PAIR] / (o[:, PAIR:] + probs[g][1][p])
                    mixed_ref[rows, RET_WIDTH + i * PAIR:RET_WIDTH + (i + 1) * PAIR] = (
                        att * _silu(cols(COL_AG, i))).astype(bf16)
            prev = cur
        for i in range(N_CARRY):
            carry_ref[i] = prev[i]

        out = _dot(mixed_ref[...], wout_ref[...])
        ms2 = jnp.mean(out * out, axis=-1, keepdims=True)
        o_ref[...] = xp_ref[...] + out * lax.rsqrt(ms2 + EPS) * gpost_ref[...]
        yield

    def step(proj_w, proj_r):
        live = [stage2(proj_r), stage1(proj_w)]
        while live:
            for s in list(live):
                if next(s, _DONE) is _DONE:
                    live.remove(s)

    @pl.when(n % 2 == 0)
    def _():
        step(proj_a, proj_b)

    @pl.when(n % 2 == 1)
    def _():
        step(proj_b, proj_a)


def _retention_tables():
    f32 = jnp.float32
    c = CHUNK
    log_gamma = jnp.log1p(-jnp.exp2(-5.0 - jnp.arange(RET_HEADS, dtype=f32)))
    pos = jnp.arange(c, dtype=f32)
    diff = pos[:, None] - pos[None, :]
    intra = jnp.where(diff >= 0,
                      jnp.exp(log_gamma[:, None, None] * jnp.maximum(diff, 0.0)), 0.0)
    decay = intra * (RET_DK ** -0.5)
    xi = jnp.exp(log_gamma[:, None] * (pos + 1.0))
    zeta = jnp.exp(log_gamma[:, None] * (c - 1.0 - pos)) * (RET_DK ** -0.5)
    xi_b = jnp.broadcast_to(xi[:, :, None], (RET_HEADS, c, RET_DK))
    zeta_t = jnp.broadcast_to(zeta[:, None, :], (RET_HEADS, RET_DK, c))
    chunk_decay = jnp.exp(log_gamma * c)
    return decay, xi_b, zeta_t, chunk_decay


def _alibi_bias():
    f32 = jnp.float32
    t = CHUNK
    i = jnp.arange(t)[:, None]
    j = jnp.arange(t)[None, :]
    dist = jnp.where(j <= i, i - j, i + t - j).astype(f32)
    slopes = jnp.exp2(-8.0 * (jnp.arange(ATT_HEADS, dtype=f32) + 1.0) / ATT_HEADS)
    bias = LOG2E * slopes[:, None, None] * dist[None]
    first = jnp.where((j <= i)[None], bias, BIG)
    return jnp.stack([bias, first])


def kernel(x, g_pre, w_in, sinks, w_out, g_post):
    bsz, seq, d_model = x.shape
    assert d_model == D_MODEL and w_in.shape == (D_MODEL, IN_COLS)
    assert w_out.shape == (D_MIX, D_MODEL) and seq % TM == 0
    f32, bf16 = jnp.float32, jnp.bfloat16
    decay, xi_b, zeta_t, chunk_decay = _retention_tables()
    bias = _alibi_bias()
    n_tiles = bsz * seq // TM
    x2 = x.reshape(bsz * seq, D_MODEL)

    def const(shape):
        return pl.BlockSpec(shape, lambda n: (0,) * len(shape))

    smem = pl.BlockSpec(memory_space=pltpu.SMEM)
    out = pl.pallas_call(
        functools.partial(_layer_kernel, tm=TM, tiles_per_seq=seq // TM),
        grid=(n_tiles + 1,),
        in_specs=[
            smem,
            smem,
            pl.BlockSpec((TM, D_MODEL), lambda n: (jnp.minimum(n, n_tiles - 1), 0)),
            pl.BlockSpec((TM, D_MODEL), lambda n: (jnp.maximum(n - 1, 0), 0)),
            const((1, D_MODEL)),
            const((D_MODEL, IN_COLS)),
            const((D_MIX, D_MODEL)),
            const((1, D_MODEL)),
            const((RET_HEADS, CHUNK, CHUNK)),
            const((RET_HEADS, CHUNK, RET_DK)),
            const((RET_HEADS, RET_DK, CHUNK)),
            const((2, ATT_HEADS, CHUNK, CHUNK)),
        ],
        out_specs=pl.BlockSpec((TM, D_MODEL), lambda n: (jnp.maximum(n - 1, 0), 0)),
        out_shape=jax.ShapeDtypeStruct(x2.shape, x.dtype),
        scratch_shapes=[
            pltpu.VMEM((TM, IN_COLS), f32),
            pltpu.VMEM((TM, IN_COLS), f32),
            pltpu.VMEM((RET_HEADS, RET_DK, RET_DV), f32),
            pltpu.VMEM((N_CARRY, CHUNK, CHUNK), bf16),
            pltpu.VMEM((TM, D_MIX), bf16),
            pltpu.VMEM((TM, D_MODEL), bf16),
        ],
        compiler_params=pltpu.CompilerParams(
            dimension_semantics=("arbitrary",),
            vmem_limit_bytes=VMEM_LIMIT_BYTES),
        name="hybrid_layer",
    )(sinks.astype(f32) * LOG2E, chunk_decay, x2, x2, g_pre.reshape(1, D_MODEL).astype(f32),
      w_in.astype(bf16), w_out.astype(bf16), g_post.reshape(1, D_MODEL).astype(f32),
      decay, xi_b, zeta_t, bias)
    return out.reshape(x.shape)
```

```python
import functools

import jax
import jax.numpy as jnp
from jax import lax
from jax.experimental import pallas as pl
from jax.experimental.pallas import tpu as pltpu

D_MODEL = 1024
RET_HEADS = 4
RET_DK = 128
RET_DV = 128
RET_WIDTH = RET_HEADS * RET_DV
CHUNK = 128
ATT_HEADS = 8
ATT_KV_HEADS = 2
ATT_GROUP = ATT_HEADS // ATT_KV_HEADS
ATT_HEAD_DIM = 64
ATT_WIDTH = ATT_HEADS * ATT_HEAD_DIM
ATT_KV_WIDTH = ATT_KV_HEADS * ATT_HEAD_DIM
D_MIX = RET_WIDTH + ATT_WIDTH
EPS = 1e-6
BIG = 1e30
LOG2E = 1.4426950408889634

COL_RQ = 0
COL_RK = COL_RQ + RET_HEADS * RET_DK
COL_RV = COL_RK + RET_HEADS * RET_DK
COL_RG = COL_RV + RET_WIDTH
COL_AQ = COL_RG + RET_WIDTH
COL_AK = COL_AQ + ATT_WIDTH
COL_AV = COL_AK + ATT_KV_WIDTH
COL_AG = COL_AV + ATT_KV_WIDTH
IN_COLS = COL_AG + ATT_WIDTH

PAIR = 2 * ATT_HEAD_DIM
assert PAIR == CHUNK and ATT_KV_WIDTH == CHUNK and RET_DK == CHUNK and RET_DV == CHUNK

TM = 512
VMEM_LIMIT_BYTES = 56 * 1024 * 1024
N_CARRY = 5
PROJ_COLS = 256
assert IN_COLS % PROJ_COLS == 0
_DONE = object()


def _silu(x):
    hx = 0.5 * x
    return hx + hx * jnp.tanh(hx)


def _dot(a, b):
    return jnp.dot(a, b, preferred_element_type=jnp.float32)


def _retention_scores(q, k):
    k_t = k.T
    return _dot(q.astype(jnp.bfloat16), k_t.astype(jnp.bfloat16)), k_t


def _retention_mix(hd, sc, q, k_t, v, state_ref, cdecay, decay_ref, xi_ref, zeta_t_ref):
    bf16 = jnp.bfloat16
    vb = v.astype(bf16)
    st = state_ref[hd]
    lhs = jnp.concatenate([(sc * decay_ref[hd]).astype(bf16), (q * xi_ref[hd]).astype(bf16)], axis=1)
    rhs = jnp.concatenate([vb, st.astype(bf16)], axis=0)
    o = _dot(lhs, rhs)
    state_ref[hd] = cdecay * st + _dot((k_t * zeta_t_ref[hd]).astype(bf16), vb)
    return o


def _retention_norm_gate(o, gate):
    mu = jnp.mean(o, axis=-1, keepdims=True)
    d = o - mu
    var = jnp.mean(d * d, axis=-1, keepdims=True)
    return d * lax.rsqrt(var + EPS) * _silu(gate)


def _attention_prep(k_both, v_both, lo):
    bf16 = jnp.bfloat16
    k_t = (k_both.T * (ATT_HEAD_DIM ** -0.5 * LOG2E)).astype(bf16)
    v_roll = pltpu.roll(v_both, ATT_HEAD_DIM, axis=1)
    return (k_t,
            jnp.where(lo, v_both, 0.0).astype(bf16),
            jnp.where(lo, 0.0, v_roll).astype(bf16),
            jnp.where(lo, v_roll, 0.0).astype(bf16),
            jnp.where(lo, 0.0, v_both).astype(bf16))


def _attention_scores(g, q_g, cur, prev):
    hs = slice(g * ATT_HEAD_DIM, (g + 1) * ATT_HEAD_DIM)
    k_win = jnp.concatenate([cur[0][hs], prev[0][hs]], axis=1)
    zeros = jnp.zeros_like(k_win)
    rhs_even = jnp.concatenate([k_win, zeros], axis=0)
    rhs_odd = jnp.concatenate([zeros, k_win], axis=0)
    lhs = jnp.concatenate([q_g[:, :PAIR], q_g[:, PAIR:]], axis=0)
    return _dot(lhs, rhs_even), _dot(lhs, rhs_odd)


def _attention_probs(g, s_par, bias_ref, first_idx, sinks_ref, tri, lo):
    bf16 = jnp.bfloat16
    p_rows, sink_terms = [], []
    for p in range(2):
        parts, sink_e = [], []
        for e in range(2):
            hq = g * ATT_GROUP + p * 2 + e
            s = s_par[e][p * CHUNK:(p + 1) * CHUNK]
            sf = jnp.where(tri, s[:, :CHUNK], s[:, CHUNK:]) - bias_ref[first_idx, hq]
            m = jnp.max(sf, axis=-1, keepdims=True)
            ex = jnp.exp2(sf - m)
            sink_e.append(jnp.exp2(sinks_ref[hq] - m))
            parts += [jnp.where(tri, ex, 0.0).astype(bf16), jnp.where(tri, 0.0, ex).astype(bf16)]
        p_rows.append(jnp.concatenate(parts, axis=1))
        sink_terms.append(jnp.where(lo, sink_e[0], sink_e[1]))
    return jnp.concatenate(p_rows, axis=0), sink_terms


def _attention_pv(g, probs, cur, prev, lo):
    bf16 = jnp.bfloat16
    ones_l = jnp.where(lo, 1.0, 0.0).astype(bf16)
    ones_r = jnp.where(lo, 0.0, 1.0).astype(bf16)
    v_l, v_r = cur[1 + 2 * g], cur[2 + 2 * g]
    pv_l, pv_r = prev[1 + 2 * g], prev[2 + 2 * g]
    rhs = jnp.concatenate([jnp.concatenate([v_l, ones_l], axis=1),
                           jnp.concatenate([pv_l, ones_l], axis=1),
                           jnp.concatenate([v_r, ones_r], axis=1),
                           jnp.concatenate([pv_r, ones_r], axis=1)], axis=0)
    return _dot(probs, rhs)


def _layer_kernel(sinks_ref, cdecay_ref, xn_ref, xp_ref, gpre_ref, win_ref, wout_ref,
                  gpost_ref, decay_ref, xi_ref, zeta_t_ref, bias_ref,
                  o_ref, proj_a, proj_b, state_ref, carry_ref, mixed_ref, h_ref, *, tm,
                  tiles_per_seq):
    f32, bf16 = jnp.float32, jnp.bfloat16
    n = pl.program_id(0)
    seq_start = (jnp.maximum(n - 1, 0) % tiles_per_seq) == 0

    @pl.when(n == 0)
    def _():
        proj_b[...] = jnp.zeros_like(proj_b)

    @pl.when(seq_start)
    def _():
        state_ref[...] = jnp.zeros_like(state_ref)
        carry_ref[...] = jnp.zeros_like(carry_ref)

    row = lax.broadcasted_iota(jnp.int32, (CHUNK, CHUNK), 0)
    lane = lax.broadcasted_iota(jnp.int32, (CHUNK, CHUNK), 1)
    tri = lane <= row
    lo = lane < ATT_HEAD_DIM

    def stage1(proj_w):
        x = xn_ref[...]
        ms = jnp.mean(x * x, axis=-1, keepdims=True)
        h_ref[...] = (x * lax.rsqrt(ms + EPS) * gpre_ref[...]).astype(bf16)
        yield
        for j in range(0, IN_COLS, PROJ_COLS):
            proj_w[:, j:j + PROJ_COLS] = _dot(h_ref[...], win_ref[:, j:j + PROJ_COLS])
            yield

    def stage2(proj_r):
        prev = tuple(carry_ref[i] for i in range(N_CARRY))
        for c in range(tm // CHUNK):
            rows = slice(c * CHUNK, (c + 1) * CHUNK)

            def cols(base, i, width=CHUNK):
                return proj_r[rows, base + i * width:base + (i + 1) * width]

            heads = range(RET_HEADS)
            scored = [_retention_scores(cols(COL_RQ, hd), cols(COL_RK, hd)) for hd in heads]
            yield
            ret = [_retention_mix(hd, scored[hd][0], cols(COL_RQ, hd), scored[hd][1],
                                  cols(COL_RV, hd), state_ref, cdecay_ref[hd], decay_ref,
                                  xi_ref, zeta_t_ref) for hd in heads]
            yield
            first_idx = seq_start.astype(jnp.int32) if c == 0 else 0
            cur = _attention_prep(cols(COL_AK, 0), cols(COL_AV, 0), lo)
            groups = range(ATT_KV_HEADS)
            s_par = [_attention_scores(g, cols(COL_AQ, g, 2 * PAIR).astype(bf16), cur, prev)
                     for g in groups]
            yield
            for hd in heads:
                mixed_ref[rows, hd * RET_DV:(hd + 1) * RET_DV] = _retention_norm_gate(
                    ret[hd], cols(COL_RG, hd)).astype(bf16)
            probs = [_attention_probs(g, s_par[g], bias_ref, first_idx, sinks_ref, tri, lo)
                     for g in groups]
            yield
            pv = [_attention_pv(g, probs[g][0], cur, prev, lo) for g in groups]
            yield
            for g in groups:
                for p in range(2):
                    i = g * 2 + p
                    o = pv[g][p * CHUNK:(p + 1) * CHUNK]
                    att = o[:, :PAIR] / (o[:, PAIR:] + probs[g][1][p])
                    mixed_ref[rows, RET_WIDTH + i * PAIR:RET_WIDTH + (i + 1) * PAIR] = (
                        att * _silu(cols(COL_AG, i))).astype(bf16)
            prev = cur
        for i in range(N_CARRY):
            carry_ref[i] = prev[i]

        out = _dot(mixed_ref[...], wout_ref[...])
        ms2 = jnp.mean(out * out, axis=-1, keepdims=True)
        o_ref[...] = xp_ref[...] + out * lax.rsqrt(ms2 + EPS) * gpost_ref[...]
        yield

    def step(proj_w, proj_r):
        live = [stage2(proj_r), stage1(proj_w)]
        while live:
            for s in list(live):
                if next(s, _DONE) is _DONE:
                    live.remove(s)

    @pl.when(n % 2 == 0)
    def _():
        step(proj_a, proj_b)

    @pl.when(n % 2 == 1)
    def _():
        step(proj_b, proj_a)


def _retention_tables():
    f32 = jnp.float32
    c = CHUNK
    log_gamma = jnp.log1p(-jnp.exp2(-5.0 - jnp.arange(RET_HEADS, dtype=f32)))
    pos = jnp.arange(c, dtype=f32)
    diff = pos[:, None] - pos[None, :]
    intra = jnp.where(diff >= 0,
                      jnp.exp(log_gamma[:, None, None] * jnp.maximum(diff, 0.0)), 0.0)
    decay = intra * (RET_DK ** -0.5)
    xi = jnp.exp(log_gamma[:, None] * (pos + 1.0))
    zeta = jnp.exp(log_gamma[:, None] * (c - 1.0 - pos)) * (RET_DK ** -0.5)
    xi_b = jnp.broadcast_to(xi[:, :, None], (RET_HEADS, c, RET_DK))
    zeta_t = jnp.broadcast_to(zeta[:, None, :], (RET_HEADS, RET_DK, c))
    chunk_decay = jnp.exp(log_gamma * c)
    return decay, xi_b, zeta_t, chunk_decay


def _alibi_bias():
    f32 = jnp.float32
    t = CHUNK
    i = jnp.arange(t)[:, None]
    j = jnp.arange(t)[None, :]
    dist = jnp.where(j <= i, i - j, i + t - j).astype(f32)
    slopes = jnp.exp2(-8.0 * (jnp.arange(ATT_HEADS, dtype=f32) + 1.0) / ATT_HEADS)
    bias = LOG2E * slopes[:, None, None] * dist[None]
    first = jnp.where((j <= i)[None], bias, BIG)
    return jnp.stack([bias, first])


def kernel(x, g_pre, w_in, sinks, w_out, g_post):
    bsz, seq, d_model = x.shape
    assert d_model == D_MODEL and w_in.shape == (D_MODEL, IN_COLS)
    assert w_out.shape == (D_MIX, D_MODEL) and seq % TM == 0
    f32, bf16 = jnp.float32, jnp.bfloat16
    decay, xi_b, zeta_t, chunk_decay = _retention_tables()
    bias = _alibi_bias()
    n_tiles = bsz * seq // TM
    x2 = x.reshape(bsz * seq, D_MODEL)

    def const(shape):
        return pl.BlockSpec(shape, lambda n: (0,) * len(shape))

    smem = pl.BlockSpec(memory_space=pltpu.SMEM)
    out = pl.pallas_call(
        functools.partial(_layer_kernel, tm=TM, tiles_per_seq=seq // TM),
        grid=(n_tiles + 1,),
        in_specs=[
            smem,
            smem,
            pl.BlockSpec((TM, D_MODEL), lambda n: (jnp.minimum(n, n_tiles - 1), 0)),
            pl.BlockSpec((TM, D_MODEL), lambda n: (jnp.maximum(n - 1, 0), 0)),
            const((1, D_MODEL)),
            const((D_MODEL, IN_COLS)),
            const((D_MIX, D_MODEL)),
            const((1, D_MODEL)),
            const((RET_HEADS, CHUNK, CHUNK)),
            const((RET_HEADS, CHUNK, RET_DK)),
            const((RET_HEADS, RET_DK, CHUNK)),
            const((2, ATT_HEADS, CHUNK, CHUNK)),
        ],
        out_specs=pl.BlockSpec((TM, D_MODEL), lambda n: (jnp.maximum(n - 1, 0), 0)),
        out_shape=jax.ShapeDtypeStruct(x2.shape, x.dtype),
        scratch_shapes=[
            pltpu.VMEM((TM, IN_COLS), f32),
            pltpu.VMEM((TM, IN_COLS), f32),
            pltpu.VMEM((RET_HEADS, RET_DK, RET_DV), f32),
            pltpu.VMEM((N_CARRY, CHUNK, CHUNK), bf16),
            pltpu.VMEM((TM, D_MIX), bf16),
            pltpu.VMEM((TM, D_MODEL), bf16),
        ],
        compiler_params=pltpu.CompilerParams(
            dimension_semantics=("arbitrary",),
            vmem_limit_bytes=VMEM_LIMIT_BYTES),
        name="hybrid_layer",
    )(sinks.astype(f32) * LOG2E, chunk_decay, x2, x2, g_pre.reshape(1, D_MODEL).astype(f32),
      w_in.astype(bf16), w_out.astype(bf16), g_post.reshape(1, D_MODEL).astype(f32),
      decay, xi_b, zeta_t, bias)
    return out.reshape(x.shape)
```

```python
import functools

import jax
import jax.numpy as jnp
from jax import lax
from jax.experimental import pallas as pl
from jax.experimental.pallas import tpu as pltpu

D_MODEL = 1024
RET_HEADS = 4
RET_DK = 128
RET_DV = 128
RET_WIDTH = RET_HEADS * RET_DV
CHUNK = 128
ATT_HEADS = 8
ATT_KV_HEADS = 2
ATT_GROUP = ATT_HEADS // ATT_KV_HEADS
ATT_HEAD_DIM = 64
ATT_WIDTH = ATT_HEADS * ATT_HEAD_DIM
ATT_KV_WIDTH = ATT_KV_HEADS * ATT_HEAD_DIM
D_MIX = RET_WIDTH + ATT_WIDTH
EPS = 1e-6
BIG = 1e30
LOG2E = 1.4426950408889634

COL_RQ = 0
COL_RK = COL_RQ + RET_HEADS * RET_DK
COL_RV = COL_RK + RET_HEADS * RET_DK
COL_RG = COL_RV + RET_WIDTH
COL_AQ = COL_RG + RET_WIDTH
COL_AK = COL_AQ + ATT_WIDTH
COL_AV = COL_AK + ATT_KV_WIDTH
COL_AG = COL_AV + ATT_KV_WIDTH
IN_COLS = COL_AG + ATT_WIDTH

PAIR = 2 * ATT_HEAD_DIM
assert PAIR == CHUNK and ATT_KV_WIDTH == CHUNK and RET_DK == CHUNK and RET_DV == CHUNK

TM = 256
VMEM_LIMIT_BYTES = 56 * 1024 * 1024
N_CARRY = 5
PROJ_COLS = 256
PIECES_PER_PHASE = 1
NORMALISE_AFTER_PHASE = 4
assert IN_COLS % PROJ_COLS == 0


def _silu(x):
    hx = 0.5 * x
    return hx + hx * jnp.tanh(hx)


_dot = functools.partial(jnp.dot, preferred_element_type=jnp.float32)


def _retention_pair_scores(q_pair, k_pair):
    bf16 = jnp.bfloat16
    k_ts = (k_pair[:, :RET_DK].T, k_pair[:, RET_DK:].T)
    zeros = jnp.zeros((RET_DK, CHUNK), bf16)
    rhs = jnp.concatenate([jnp.concatenate([k_ts[0].astype(bf16), zeros], axis=1),
                           jnp.concatenate([zeros, k_ts[1].astype(bf16)], axis=1)], axis=0)
    return _dot(q_pair.astype(bf16), rhs), k_ts


def _retention_mix(sc, q, v, st, decay, xi):
    bf16 = jnp.bfloat16
    lhs = jnp.concatenate([(sc * decay).astype(bf16), (q * xi).astype(bf16)], axis=1)
    rhs = jnp.concatenate([v.astype(bf16), st.astype(bf16)], axis=0)
    return _dot(lhs, rhs)


def _retention_pair_kv(k_ts, v_pair, zeta_a, zeta_b):
    bf16 = jnp.bfloat16
    lhs = jnp.concatenate([(k_ts[0] * zeta_a).astype(bf16), (k_ts[1] * zeta_b).astype(bf16)], axis=1)
    vb = v_pair.astype(bf16)
    zeros = jnp.zeros((CHUNK, RET_DV), bf16)
    rhs = jnp.concatenate([jnp.concatenate([vb[:, :RET_DV], zeros], axis=1),
                           jnp.concatenate([zeros, vb[:, RET_DV:]], axis=1)], axis=0)
    return _dot(lhs, rhs)


def _retention_norm_gate(o, gate):
    mu = jnp.mean(o, axis=-1, keepdims=True)
    d = o - mu
    var = jnp.mean(d * d, axis=-1, keepdims=True)
    return d * lax.rsqrt(var + EPS) * _silu(gate)


def _attention_prep(k_both, v_both, lo):
    bf16 = jnp.bfloat16
    k_t = (k_both.T * (ATT_HEAD_DIM ** -0.5 * LOG2E)).astype(bf16)
    v_roll = pltpu.roll(v_both, ATT_HEAD_DIM, axis=1)
    return (k_t,
            jnp.where(lo, v_both, 0.0).astype(bf16),
            jnp.where(lo, 0.0, v_roll).astype(bf16),
            jnp.where(lo, v_roll, 0.0).astype(bf16),
            jnp.where(lo, 0.0, v_both).astype(bf16))


def _attention_scores(g, q_g, cur, prev):
    hs = slice(g * ATT_HEAD_DIM, (g + 1) * ATT_HEAD_DIM)
    k_win = jnp.concatenate([cur[0][hs], prev[0][hs]], axis=1)
    zeros = jnp.zeros_like(k_win)
    rhs_even = jnp.concatenate([k_win, zeros], axis=0)
    rhs_odd = jnp.concatenate([zeros, k_win], axis=0)
    lhs = jnp.concatenate([q_g[:, :PAIR], q_g[:, PAIR:]], axis=0)
    return _dot(lhs, rhs_even), _dot(lhs, rhs_odd)


def _attention_probs(g, s_par, bias_ref, first_idx, sinks_ref, tri, lo):
    bf16 = jnp.bfloat16
    p_rows, inv_dens = [], []
    for p in range(2):
        parts, inv_e = [], []
        for e in range(2):
            hq = g * ATT_GROUP + p * 2 + e
            s = s_par[e][p * CHUNK:(p + 1) * CHUNK]
            sf = jnp.where(tri, s[:, :CHUNK], s[:, CHUNK:]) - bias_ref[first_idx, hq]
            m = jnp.max(sf, axis=-1, keepdims=True)
            ex = jnp.exp2(sf - m)
            den = jnp.sum(ex, axis=-1, keepdims=True) + jnp.exp2(sinks_ref[hq] - m)
            inv_e.append(1.0 / den)
            parts += [jnp.where(tri, ex, 0.0).astype(bf16), jnp.where(tri, 0.0, ex).astype(bf16)]
        p_rows.append(jnp.concatenate(parts, axis=1))
        inv_dens.append(jnp.where(lo, inv_e[0], inv_e[1]))
    return jnp.concatenate(p_rows, axis=0), inv_dens


def _attention_pv(g, probs, cur, prev):
    v_l, v_r = cur[1 + 2 * g], cur[2 + 2 * g]
    pv_l, pv_r = prev[1 + 2 * g], prev[2 + 2 * g]
    return _dot(probs, jnp.concatenate([v_l, pv_l, v_r, pv_r], axis=0))


def _layer_kernel(sinks_ref, cdecay_ref, x0_ref, xnext_ref, xp_ref, gpre_ref, win_ref, wout_ref,
                  gpost_ref, decay_ref, xi_ref, zeta_t_ref, bias_ref,
                  o_ref, proj_a, proj_b, h_a, h_b, state_ref, carry_ref, mixed_ref, *, tm,
                  tiles_per_seq):
    f32, bf16 = jnp.float32, jnp.bfloat16
    n = pl.program_id(0)
    seq_start = (jnp.maximum(n - 1, 0) % tiles_per_seq) == 0

    def normalise(x_ref, h_ref):
        x = x_ref[...]
        ms = jnp.mean(x * x, axis=-1, keepdims=True)
        h_ref[...] = (x * lax.rsqrt(ms + EPS) * gpre_ref[...]).astype(bf16)

    @pl.when(n == 0)
    def _():
        proj_b[...] = jnp.zeros_like(proj_b)
        normalise(x0_ref, h_a)

    @pl.when(seq_start)
    def _():
        state_ref[...] = jnp.zeros_like(state_ref)
        carry_ref[...] = jnp.zeros_like(carry_ref)

    row = lax.broadcasted_iota(jnp.int32, (CHUNK, CHUNK), 0)
    lane = lax.broadcasted_iota(jnp.int32, (CHUNK, CHUNK), 1)
    tri = lane <= row
    lo = lane < ATT_HEAD_DIM

    def project(h_ref, proj_w):
        for j in range(0, IN_COLS, PROJ_COLS):
            proj_w[:, j:j + PROJ_COLS] = _dot(h_ref[...], win_ref[:, j:j + PROJ_COLS])
            yield

    def mix(proj_r):
        prev = tuple(carry_ref[i] for i in range(N_CARRY))
        pairs, groups = range(RET_HEADS // 2), range(ATT_KV_HEADS)
        for c in range(tm // CHUNK):
            rows = slice(c * CHUNK, (c + 1) * CHUNK)

            def cols(base, i, width=CHUNK):
                return proj_r[rows, base + i * width:base + (i + 1) * width]

            scored = [_retention_pair_scores(cols(COL_RQ, hp, 2 * RET_DK), cols(COL_RK, hp, 2 * RET_DK))
                      for hp in pairs]
            cur = _attention_prep(cols(COL_AK, 0), cols(COL_AV, 0), lo)
            s_par = [_attention_scores(g, cols(COL_AQ, g, 2 * PAIR).astype(bf16), cur, prev)
                     for g in groups]
            yield
            ret = []
            for hd in range(RET_HEADS):
                sc = scored[hd // 2][0][:, (hd % 2) * CHUNK:(hd % 2 + 1) * CHUNK]
                ret.append(_retention_mix(sc, cols(COL_RQ, hd), cols(COL_RV, hd), state_ref[hd],
                                          decay_ref[hd], xi_ref[hd]))
            for hp in pairs:
                kv = _retention_pair_kv(scored[hp][1], cols(COL_RV, hp, 2 * RET_DV),
                                        zeta_t_ref[2 * hp], zeta_t_ref[2 * hp + 1])
                for e in range(2):
                    hd = 2 * hp + e
                    state_ref[hd] = cdecay_ref[hd] * state_ref[hd] + kv[:, e * RET_DV:(e + 1) * RET_DV]
            yield
            for hd in range(RET_HEADS):
                mixed_ref[rows, hd * RET_DV:(hd + 1) * RET_DV] = _retention_norm_gate(
                    ret[hd], cols(COL_RG, hd)).astype(bf16)
            first_idx = seq_start.astype(jnp.int32) if c == 0 else 0
            probs = [_attention_probs(g, s_par[g], bias_ref, first_idx, sinks_ref, tri, lo)
                     for g in groups]
            yield
            pv = [_attention_pv(g, probs[g][0], cur, prev) for g in groups]
            yield
            for g in groups:
                for p in range(2):
                    i = g * 2 + p
                    att = pv[g][p * CHUNK:(p + 1) * CHUNK] * probs[g][1][p]
                    mixed_ref[rows, RET_WIDTH + i * PAIR:RET_WIDTH + (i + 1) * PAIR] = (
                        att * _silu(cols(COL_AG, i))).astype(bf16)
            prev = cur
            yield
        for i in range(N_CARRY):
            carry_ref[i] = prev[i]

        out = _dot(mixed_ref[...], wout_ref[...])
        ms2 = jnp.mean(out * out, axis=-1, keepdims=True)
        o_ref[...] = xp_ref[...] + out * lax.rsqrt(ms2 + EPS) * gpost_ref[...]
        yield

    def step(h_cur, h_next, proj_w, proj_r):
        pieces = project(h_cur, proj_w)
        next(pieces)
        for i, _ in enumerate(mix(proj_r)):
            for _ in range(PIECES_PER_PHASE):
                next(pieces, None)
            if i == NORMALISE_AFTER_PHASE:
                normalise(xnext_ref, h_next)
        for _ in pieces:
            pass

    @pl.when(n % 2 == 0)
    def _():
        step(h_a, h_b, proj_a, proj_b)

    @pl.when(n % 2 == 1)
    def _():
        step(h_b, h_a, proj_b, proj_a)


def _retention_tables():
    f32 = jnp.float32
    c = CHUNK
    log_gamma = jnp.log1p(-jnp.exp2(-5.0 - jnp.arange(RET_HEADS, dtype=f32)))
    pos = jnp.arange(c, dtype=f32)
    diff = pos[:, None] - pos[None, :]
    intra = jnp.where(diff >= 0,
                      jnp.exp(log_gamma[:, None, None] * jnp.maximum(diff, 0.0)), 0.0)
    decay = intra * (RET_DK ** -0.5)
    xi = jnp.exp(log_gamma[:, None] * (pos + 1.0))
    zeta = jnp.exp(log_gamma[:, None] * (c - 1.0 - pos)) * (RET_DK ** -0.5)
    xi_b = jnp.broadcast_to(xi[:, :, None], (RET_HEADS, c, RET_DK))
    zeta_t = jnp.broadcast_to(zeta[:, None, :], (RET_HEADS, RET_DK, c))
    chunk_decay = jnp.exp(log_gamma * c)
    return decay, xi_b, zeta_t, chunk_decay


def _alibi_bias():
    f32 = jnp.float32
    t = CHUNK
    i = jnp.arange(t)[:, None]
    j = jnp.arange(t)[None, :]
    dist = jnp.where(j <= i, i - j, i + t - j).astype(f32)
    slopes = jnp.exp2(-8.0 * (jnp.arange(ATT_HEADS, dtype=f32) + 1.0) / ATT_HEADS)
    bias = LOG2E * slopes[:, None, None] * dist[None]
    first = jnp.where((j <= i)[None], bias, BIG)
    return jnp.stack([bias, first])


def kernel(x, g_pre, w_in, sinks, w_out, g_post):
    bsz, seq, d_model = x.shape
    assert d_model == D_MODEL and w_in.shape == (D_MODEL, IN_COLS)
    assert w_out.shape == (D_MIX, D_MODEL) and seq % TM == 0
    f32, bf16 = jnp.float32, jnp.bfloat16
    decay, xi_b, zeta_t, chunk_decay = _retention_tables()
    bias = _alibi_bias()
    n_tiles = bsz * seq // TM
    x2 = x.reshape(bsz * seq, D_MODEL)

    def const(shape):
        return pl.BlockSpec(shape, lambda n: (0,) * len(shape))

    smem = pl.BlockSpec(memory_space=pltpu.SMEM)
    out = pl.pallas_call(
        functools.partial(_layer_kernel, tm=TM, tiles_per_seq=seq // TM),
        grid=(n_tiles + 1,),
        in_specs=[
            smem,
            smem,
            pl.BlockSpec((TM, D_MODEL), lambda n: (0, 0)),
            pl.BlockSpec((TM, D_MODEL), lambda n: (jnp.minimum(n + 1, n_tiles - 1), 0)),
            pl.BlockSpec((TM, D_MODEL), lambda n: (jnp.maximum(n - 1, 0), 0)),
            const((1, D_MODEL)),
            const((D_MODEL, IN_COLS)),
            const((D_MIX, D_MODEL)),
            const((1, D_MODEL)),
            const((RET_HEADS, CHUNK, CHUNK)),
            const((RET_HEADS, CHUNK, RET_DK)),
            const((RET_HEADS, RET_DK, CHUNK)),
            const((2, ATT_HEADS, CHUNK, CHUNK)),
        ],
        out_specs=pl.BlockSpec((TM, D_MODEL), lambda n: (jnp.maximum(n - 1, 0), 0)),
        out_shape=jax.ShapeDtypeStruct(x2.shape, x.dtype),
        scratch_shapes=[
            pltpu.VMEM((TM, IN_COLS), f32),
            pltpu.VMEM((TM, IN_COLS), f32),
            pltpu.VMEM((TM, D_MODEL), bf16),
            pltpu.VMEM((TM, D_MODEL), bf16),
            pltpu.VMEM((RET_HEADS, RET_DK, RET_DV), f32),
            pltpu.VMEM((N_CARRY, CHUNK, CHUNK), bf16),
            pltpu.VMEM((TM, D_MIX), bf16),
        ],
        compiler_params=pltpu.CompilerParams(
            dimension_semantics=("arbitrary",),
            vmem_limit_bytes=VMEM_LIMIT_BYTES),
        name="hybrid_layer",
    )(sinks.astype(f32) * LOG2E, chunk_decay, x2, x2, x2, g_pre.reshape(1, D_MODEL).astype(f32),
      w_in.astype(bf16), w_out.astype(bf16), g_post.reshape(1, D_MODEL).astype(f32),
      decay, xi_b, zeta_t, bias)
    return out.reshape(x.shape)
```

```python
import functools

import jax
import jax.numpy as jnp
from jax import lax
from jax.experimental import pallas as pl
from jax.experimental.pallas import tpu as pltpu

D_MODEL = 1024
RET_HEADS = 4
RET_DK = 128
RET_DV = 128
RET_WIDTH = RET_HEADS * RET_DV
CHUNK = 128
ATT_HEADS = 8
ATT_KV_HEADS = 2
ATT_GROUP = ATT_HEADS // ATT_KV_HEADS
ATT_HEAD_DIM = 64
ATT_WIDTH = ATT_HEADS * ATT_HEAD_DIM
ATT_KV_WIDTH = ATT_KV_HEADS * ATT_HEAD_DIM
D_MIX = RET_WIDTH + ATT_WIDTH
EPS = 1e-6
BIG = 1e30
LOG2E = 1.4426950408889634

COL_RQ = 0
COL_RK = COL_RQ + RET_HEADS * RET_DK
COL_RV = COL_RK + RET_HEADS * RET_DK
COL_RG = COL_RV + RET_WIDTH
COL_AQ = COL_RG + RET_WIDTH
COL_AK = COL_AQ + ATT_WIDTH
COL_AV = COL_AK + ATT_KV_WIDTH
COL_AG = COL_AV + ATT_KV_WIDTH
IN_COLS = COL_AG + ATT_WIDTH

PAIR = 2 * ATT_HEAD_DIM
assert PAIR == CHUNK and ATT_KV_WIDTH == CHUNK and RET_DK == CHUNK and RET_DV == CHUNK

TM = 256
VMEM_LIMIT_BYTES = 56 * 1024 * 1024
N_CARRY = 5
PROJ_COLS = 256
PIECES_PER_PHASE = 1
NORMALISE_AFTER_PHASE = 4
assert IN_COLS % PROJ_COLS == 0


def _silu(x):
    hx = 0.5 * x
    return hx + hx * jnp.tanh(hx)


_dot = functools.partial(jnp.dot, preferred_element_type=jnp.float32)


def _retention_pair_scores(q_pair, k_pair):
    bf16 = jnp.bfloat16
    k_ts = (k_pair[:, :RET_DK].T, k_pair[:, RET_DK:].T)
    zeros = jnp.zeros((RET_DK, CHUNK), bf16)
    rhs = jnp.concatenate([jnp.concatenate([k_ts[0].astype(bf16), zeros], axis=1),
                           jnp.concatenate([zeros, k_ts[1].astype(bf16)], axis=1)], axis=0)
    return _dot(q_pair.astype(bf16), rhs), k_ts


def _retention_mix(sc, q, v, st, decay, xi):
    bf16 = jnp.bfloat16
    lhs = jnp.concatenate([(sc * decay).astype(bf16), (q * xi).astype(bf16)], axis=1)
    rhs = jnp.concatenate([v.astype(bf16), st.astype(bf16)], axis=0)
    return _dot(lhs, rhs)


def _retention_pair_kv(k_ts, v_pair, zeta_a, zeta_b):
    bf16 = jnp.bfloat16
    lhs = jnp.concatenate([(k_ts[0] * zeta_a).astype(bf16), (k_ts[1] * zeta_b).astype(bf16)], axis=1)
    vb = v_pair.astype(bf16)
    zeros = jnp.zeros((CHUNK, RET_DV), bf16)
    rhs = jnp.concatenate([jnp.concatenate([vb[:, :RET_DV], zeros], axis=1),
                           jnp.concatenate([zeros, vb[:, RET_DV:]], axis=1)], axis=0)
    return _dot(lhs, rhs)


def _retention_norm_gate(o, gate):
    mu = jnp.mean(o, axis=-1, keepdims=True)
    d = o - mu
    var = jnp.mean(d * d, axis=-1, keepdims=True)
    return d * lax.rsqrt(var + EPS) * _silu(gate)


def _attention_prep(k_both, v_both, lo):
    bf16 = jnp.bfloat16
    k_t = (k_both.T * (ATT_HEAD_DIM ** -0.5 * LOG2E)).astype(bf16)
    v_roll = pltpu.roll(v_both, ATT_HEAD_DIM, axis=1)
    return (k_t,
            jnp.where(lo, v_both, 0.0).astype(bf16),
            jnp.where(lo, 0.0, v_roll).astype(bf16),
            jnp.where(lo, v_roll, 0.0).astype(bf16),
            jnp.where(lo, 0.0, v_both).astype(bf16))


def _attention_scores(g, q_g, cur, prev):
    hs = slice(g * ATT_HEAD_DIM, (g + 1) * ATT_HEAD_DIM)
    k_win = jnp.concatenate([cur[0][hs], prev[0][hs]], axis=1)
    zeros = jnp.zeros_like(k_win)
    rhs_even = jnp.concatenate([k_win, zeros], axis=0)
    rhs_odd = jnp.concatenate([zeros, k_win], axis=0)
    lhs = jnp.concatenate([q_g[:, :PAIR], q_g[:, PAIR:]], axis=0)
    return _dot(lhs, rhs_even), _dot(lhs, rhs_odd)


def _attention_probs(g, s_par, bias_ref, first_idx, sinks_ref, tri, lo):
    bf16 = jnp.bfloat16
    p_rows, inv_dens = [], []
    for p in range(2):
        parts, inv_e = [], []
        for e in range(2):
            hq = g * ATT_GROUP + p * 2 + e
            s = s_par[e][p * CHUNK:(p + 1) * CHUNK]
            sf = jnp.where(tri, s[:, :CHUNK], s[:, CHUNK:]) - bias_ref[first_idx, hq]
            m = jnp.max(sf, axis=-1, keepdims=True)
            ex = jnp.exp2(sf - m)
            inv_e.append(jnp.exp2(sinks_ref[hq] - m))
            parts += [jnp.where(tri, ex, 0.0).astype(bf16), jnp.where(tri, 0.0, ex).astype(bf16)]
        p_rows.append(jnp.concatenate(parts, axis=1))
        inv_dens.append(jnp.where(lo, inv_e[0], inv_e[1]))
    return jnp.concatenate(p_rows, axis=0), inv_dens


def _attention_pv(g, probs, cur, prev, lo):
    bf16 = jnp.bfloat16
    ones_l = jnp.where(lo, 1.0, 0.0).astype(bf16)
    ones_r = jnp.where(lo, 0.0, 1.0).astype(bf16)
    v_l, v_r = cur[1 + 2 * g], cur[2 + 2 * g]
    pv_l, pv_r = prev[1 + 2 * g], prev[2 + 2 * g]
    rhs = jnp.concatenate([jnp.concatenate([v_l, ones_l], axis=1),
                           jnp.concatenate([pv_l, ones_l], axis=1),
                           jnp.concatenate([v_r, ones_r], axis=1),
                           jnp.concatenate([pv_r, ones_r], axis=1)], axis=0)
    return _dot(probs, rhs)


def _layer_kernel(sinks_ref, cdecay_ref, x0_ref, xnext_ref, xp_ref, gpre_ref, win_ref, wout_ref,
                  gpost_ref, decay_ref, xi_ref, zeta_t_ref, bias_ref,
                  o_ref, proj_a, proj_b, h_a, h_b, state_ref, carry_ref, mixed_ref, *, tm,
                  tiles_per_seq):
    f32, bf16 = jnp.float32, jnp.bfloat16
    n = pl.program_id(0)
    seq_start = (jnp.maximum(n - 1, 0) % tiles_per_seq) == 0

    def normalise(x_ref, h_ref):
        x = x_ref[...]
        ms = jnp.mean(x * x, axis=-1, keepdims=True)
        h_ref[...] = (x * lax.rsqrt(ms + EPS) * gpre_ref[...]).astype(bf16)

    @pl.when(n == 0)
    def _():
        proj_b[...] = jnp.zeros_like(proj_b)
        normalise(x0_ref, h_a)

    @pl.when(seq_start)
    def _():
        state_ref[...] = jnp.zeros_like(state_ref)
        carry_ref[...] = jnp.zeros_like(carry_ref)

    row = lax.broadcasted_iota(jnp.int32, (CHUNK, CHUNK), 0)
    lane = lax.broadcasted_iota(jnp.int32, (CHUNK, CHUNK), 1)
    tri = lane <= row
    lo = lane < ATT_HEAD_DIM

    def project(h_ref, proj_w):
        for j in range(0, IN_COLS, PROJ_COLS):
            proj_w[:, j:j + PROJ_COLS] = _dot(h_ref[...], win_ref[:, j:j + PROJ_COLS])
            yield

    def mix(proj_r):
        prev = tuple(carry_ref[i] for i in range(N_CARRY))
        pairs, groups = range(RET_HEADS // 2), range(ATT_KV_HEADS)
        for c in range(tm // CHUNK):
            rows = slice(c * CHUNK, (c + 1) * CHUNK)

            def cols(base, i, width=CHUNK):
                return proj_r[rows, base + i * width:base + (i + 1) * width]

            scored = [_retention_pair_scores(cols(COL_RQ, hp, 2 * RET_DK), cols(COL_RK, hp, 2 * RET_DK))
                      for hp in pairs]
            cur = _attention_prep(cols(COL_AK, 0), cols(COL_AV, 0), lo)
            s_par = [_attention_scores(g, cols(COL_AQ, g, 2 * PAIR).astype(bf16), cur, prev)
                     for g in groups]
            yield
            ret = []
            for hd in range(RET_HEADS):
                sc = scored[hd // 2][0][:, (hd % 2) * CHUNK:(hd % 2 + 1) * CHUNK]
                ret.append(_retention_mix(sc, cols(COL_RQ, hd), cols(COL_RV, hd), state_ref[hd],
                                          decay_ref[hd], xi_ref[hd]))
            for hp in pairs:
                kv = _retention_pair_kv(scored[hp][1], cols(COL_RV, hp, 2 * RET_DV),
                                        zeta_t_ref[2 * hp], zeta_t_ref[2 * hp + 1])
                for e in range(2):
                    hd = 2 * hp + e
                    state_ref[hd] = cdecay_ref[hd] * state_ref[hd] + kv[:, e * RET_DV:(e + 1) * RET_DV]
            yield
            for hd in range(RET_HEADS):
                mixed_ref[rows, hd * RET_DV:(hd + 1) * RET_DV] = _retention_norm_gate(
                    ret[hd], cols(COL_RG, hd)).astype(bf16)
            first_idx = seq_start.astype(jnp.int32) if c == 0 else 0
            probs = [_attention_probs(g, s_par[g], bias_ref, first_idx, sinks_ref, tri, lo)
                     for g in groups]
            yield
            pv = [_attention_pv(g, probs[g][0], cur, prev, lo) for g in groups]
            yield
            for g in groups:
                for p in range(2):
                    i = g * 2 + p
                    o = pv[g][p * CHUNK:(p + 1) * CHUNK]
                    att = o[:, :PAIR] / (o[:, PAIR:] + probs[g][1][p])
                    mixed_ref[rows, RET_WIDTH + i * PAIR:RET_WIDTH + (i + 1) * PAIR] = (
                        att * _silu(cols(COL_AG, i))).astype(bf16)
            prev = cur
            yield
        for i in range(N_CARRY):
            carry_ref[i] = prev[i]

        out = _dot(mixed_ref[...], wout_ref[...])
        ms2 = jnp.mean(out * out, axis=-1, keepdims=True)
        o_ref[...] = xp_ref[...] + out * lax.rsqrt(ms2 + EPS) * gpost_ref[...]
        yield

    def step(h_cur, h_next, proj_w, proj_r):
        pieces = project(h_cur, proj_w)
        next(pieces)
        for i, _ in enumerate(mix(proj_r)):
            for _ in range(PIECES_PER_PHASE):
                next(pieces, None)
            if i == NORMALISE_AFTER_PHASE:
                normalise(xnext_ref, h_next)
        for _ in pieces:
            pass

    @pl.when(n % 2 == 0)
    def _():
        step(h_a, h_b, proj_a, proj_b)

    @pl.when(n % 2 == 1)
    def _():
        step(h_b, h_a, proj_b, proj_a)


def _retention_tables():
    f32 = jnp.float32
    c = CHUNK
    log_gamma = jnp.log1p(-jnp.exp2(-5.0 - jnp.arange(RET_HEADS, dtype=f32)))
    pos = jnp.arange(c, dtype=f32)
    diff = pos[:, None] - pos[None, :]
    intra = jnp.where(diff >= 0,
                      jnp.exp(log_gamma[:, None, None] * jnp.maximum(diff, 0.0)), 0.0)
    decay = intra * (RET_DK ** -0.5)
    xi = jnp.exp(log_gamma[:, None] * (pos + 1.0))
    zeta = jnp.exp(log_gamma[:, None] * (c - 1.0 - pos)) * (RET_DK ** -0.5)
    xi_b = jnp.broadcast_to(xi[:, :, None], (RET_HEADS, c, RET_DK))
    zeta_t = jnp.broadcast_to(zeta[:, None, :], (RET_HEADS, RET_DK, c))
    chunk_decay = jnp.exp(log_gamma * c)
    return decay, xi_b, zeta_t, chunk_decay


def _alibi_bias():
    f32 = jnp.float32
    t = CHUNK
    i = jnp.arange(t)[:, None]
    j = jnp.arange(t)[None, :]
    dist = jnp.where(j <= i, i - j, i + t - j).astype(f32)
    slopes = jnp.exp2(-8.0 * (jnp.arange(ATT_HEADS, dtype=f32) + 1.0) / ATT_HEADS)
    bias = LOG2E * slopes[:, None, None] * dist[None]
    first = jnp.where((j <= i)[None], bias, BIG)
    return jnp.stack([bias, first])


def kernel(x, g_pre, w_in, sinks, w_out, g_post):
    bsz, seq, d_model = x.shape
    assert d_model == D_MODEL and w_in.shape == (D_MODEL, IN_COLS)
    assert w_out.shape == (D_MIX, D_MODEL) and seq % TM == 0
    f32, bf16 = jnp.float32, jnp.bfloat16
    decay, xi_b, zeta_t, chunk_decay = _retention_tables()
    bias = _alibi_bias()
    n_tiles = bsz * seq // TM
    x2 = x.reshape(bsz * seq, D_MODEL)

    def const(shape):
        return pl.BlockSpec(shape, lambda n: (0,) * len(shape))

    smem = pl.BlockSpec(memory_space=pltpu.SMEM)
    out = pl.pallas_call(
        functools.partial(_layer_kernel, tm=TM, tiles_per_seq=seq // TM),
        grid=(n_tiles + 1,),
        in_specs=[
            smem,
            smem,
            pl.BlockSpec((TM, D_MODEL), lambda n: (0, 0)),
            pl.BlockSpec((TM, D_MODEL), lambda n: (jnp.minimum(n + 1, n_tiles - 1), 0)),
            pl.BlockSpec((TM, D_MODEL), lambda n: (jnp.maximum(n - 1, 0), 0)),
            const((1, D_MODEL)),
            const((D_MODEL, IN_COLS)),
            const((D_MIX, D_MODEL)),
            const((1, D_MODEL)),
            const((RET_HEADS, CHUNK, CHUNK)),
            const((RET_HEADS, CHUNK, RET_DK)),
            const((RET_HEADS, RET_DK, CHUNK)),
            const((2, ATT_HEADS, CHUNK, CHUNK)),
        ],
        out_specs=pl.BlockSpec((TM, D_MODEL), lambda n: (jnp.maximum(n - 1, 0), 0)),
        out_shape=jax.ShapeDtypeStruct(x2.shape, x.dtype),
        scratch_shapes=[
            pltpu.VMEM((TM, IN_COLS), f32),
            pltpu.VMEM((TM, IN_COLS), f32),
            pltpu.VMEM((TM, D_MODEL), bf16),
            pltpu.VMEM((TM, D_MODEL), bf16),
            pltpu.VMEM((RET_HEADS, RET_DK, RET_DV), f32),
            pltpu.VMEM((N_CARRY, CHUNK, CHUNK), bf16),
            pltpu.VMEM((TM, D_MIX), bf16),
        ],
        compiler_params=pltpu.CompilerParams(
            dimension_semantics=("arbitrary",),
            vmem_limit_bytes=VMEM_LIMIT_BYTES),
        name="hybrid_layer",
    )(sinks.astype(f32) * LOG2E, chunk_decay, x2, x2, x2, g_pre.reshape(1, D_MODEL).astype(f32),
      w_in.astype(bf16), w_out.astype(bf16), g_post.reshape(1, D_MODEL).astype(f32),
      decay, xi_b, zeta_t, bias)
    return out.reshape(x.shape)
```

```python
import functools

import jax
import jax.numpy as jnp
from jax import lax
from jax.experimental import pallas as pl
from jax.experimental.pallas import tpu as pltpu

D_MODEL = 1024
RET_HEADS = 4
RET_DK = 128
RET_DV = 128
RET_WIDTH = RET_HEADS * RET_DV
CHUNK = 128
ATT_HEADS = 8
ATT_KV_HEADS = 2
ATT_GROUP = ATT_HEADS // ATT_KV_HEADS
ATT_HEAD_DIM = 64
ATT_WIDTH = ATT_HEADS * ATT_HEAD_DIM
ATT_KV_WIDTH = ATT_KV_HEADS * ATT_HEAD_DIM
D_MIX = RET_WIDTH + ATT_WIDTH
EPS = 1e-6
BIG = 1e30
LOG2E = 1.4426950408889634

COL_RQ = 0
COL_RK = COL_RQ + RET_HEADS * RET_DK
COL_RV = COL_RK + RET_HEADS * RET_DK
COL_RG = COL_RV + RET_WIDTH
COL_AQ = COL_RG + RET_WIDTH
COL_AK = COL_AQ + ATT_WIDTH
COL_AV = COL_AK + ATT_KV_WIDTH
COL_AG = COL_AV + ATT_KV_WIDTH
IN_COLS = COL_AG + ATT_WIDTH

PAIR = 2 * ATT_HEAD_DIM
assert PAIR == CHUNK and ATT_KV_WIDTH == CHUNK and RET_DK == CHUNK and RET_DV == CHUNK

TM = 512
VMEM_LIMIT_BYTES = 56 * 1024 * 1024
N_CARRY = 5
PROJ_COLS = 256
PIECES_PER_PHASE = 1
NORMALISE_AFTER_PHASE = 4
assert IN_COLS % PROJ_COLS == 0


def _silu(x):
    hx = 0.5 * x
    return hx + hx * jnp.tanh(hx)


_dot = functools.partial(jnp.dot, preferred_element_type=jnp.float32)


def _retention_pair_scores(q_pair, k_pair):
    bf16 = jnp.bfloat16
    k_ts = (k_pair[:, :RET_DK].T, k_pair[:, RET_DK:].T)
    zeros = jnp.zeros((RET_DK, CHUNK), bf16)
    rhs = jnp.concatenate([jnp.concatenate([k_ts[0].astype(bf16), zeros], axis=1),
                           jnp.concatenate([zeros, k_ts[1].astype(bf16)], axis=1)], axis=0)
    return _dot(q_pair.astype(bf16), rhs), k_ts


def _retention_mix(sc, q, v, st, decay, xi):
    bf16 = jnp.bfloat16
    lhs = jnp.concatenate([(sc * decay).astype(bf16), (q * xi).astype(bf16)], axis=1)
    rhs = jnp.concatenate([v.astype(bf16), st.astype(bf16)], axis=0)
    return _dot(lhs, rhs)


def _retention_pair_kv(k_ts, v_pair, zeta_a, zeta_b):
    bf16 = jnp.bfloat16
    lhs = jnp.concatenate([(k_ts[0] * zeta_a).astype(bf16), (k_ts[1] * zeta_b).astype(bf16)], axis=1)
    vb = v_pair.astype(bf16)
    zeros = jnp.zeros((CHUNK, RET_DV), bf16)
    rhs = jnp.concatenate([jnp.concatenate([vb[:, :RET_DV], zeros], axis=1),
                           jnp.concatenate([zeros, vb[:, RET_DV:]], axis=1)], axis=0)
    return _dot(lhs, rhs)


def _retention_norm_gate(o, gate):
    mu = jnp.mean(o, axis=-1, keepdims=True)
    d = o - mu
    var = jnp.mean(d * d, axis=-1, keepdims=True)
    return d * lax.rsqrt(var + EPS) * _silu(gate)


def _attention_prep(k_both, v_both, lo):
    bf16 = jnp.bfloat16
    k_t = (k_both.T * (ATT_HEAD_DIM ** -0.5 * LOG2E)).astype(bf16)
    v_roll = pltpu.roll(v_both, ATT_HEAD_DIM, axis=1)
    return (k_t,
            jnp.where(lo, v_both, 0.0).astype(bf16),
            jnp.where(lo, 0.0, v_roll).astype(bf16),
            jnp.where(lo, v_roll, 0.0).astype(bf16),
            jnp.where(lo, 0.0, v_both).astype(bf16))


def _attention_scores(g, q_g, cur, prev):
    hs = slice(g * ATT_HEAD_DIM, (g + 1) * ATT_HEAD_DIM)
    k_win = jnp.concatenate([cur[0][hs], prev[0][hs]], axis=1)
    zeros = jnp.zeros_like(k_win)
    rhs_even = jnp.concatenate([k_win, zeros], axis=0)
    rhs_odd = jnp.concatenate([zeros, k_win], axis=0)
    lhs = jnp.concatenate([q_g[:, :PAIR], q_g[:, PAIR:]], axis=0)
    return _dot(lhs, rhs_even), _dot(lhs, rhs_odd)


def _attention_probs(g, s_par, bias_ref, first_idx, sinks_ref, tri, lo):
    bf16 = jnp.bfloat16
    p_rows, inv_dens = [], []
    for p in range(2):
        parts, inv_e = [], []
        for e in range(2):
            hq = g * ATT_GROUP + p * 2 + e
            s = s_par[e][p * CHUNK:(p + 1) * CHUNK]
            sf = jnp.where(tri, s[:, :CHUNK], s[:, CHUNK:]) - bias_ref[first_idx, hq]
            m = jnp.max(sf, axis=-1, keepdims=True)
            ex = jnp.exp2(sf - m)
            inv_e.append(jnp.exp2(sinks_ref[hq] - m))
            parts += [jnp.where(tri, ex, 0.0).astype(bf16), jnp.where(tri, 0.0, ex).astype(bf16)]
        p_rows.append(jnp.concatenate(parts, axis=1))
        inv_dens.append(jnp.where(lo, inv_e[0], inv_e[1]))
    return jnp.concatenate(p_rows, axis=0), inv_dens


def _attention_pv(g, probs, cur, prev, lo):
    bf16 = jnp.bfloat16
    ones_l = jnp.where(lo, 1.0, 0.0).astype(bf16)
    ones_r = jnp.where(lo, 0.0, 1.0).astype(bf16)
    v_l, v_r = cur[1 + 2 * g], cur[2 + 2 * g]
    pv_l, pv_r = prev[1 + 2 * g], prev[2 + 2 * g]
    rhs = jnp.concatenate([jnp.concatenate([v_l, ones_l], axis=1),
                           jnp.concatenate([pv_l, ones_l], axis=1),
                           jnp.concatenate([v_r, ones_r], axis=1),
                           jnp.concatenate([pv_r, ones_r], axis=1)], axis=0)
    return _dot(probs, rhs)


def _layer_kernel(sinks_ref, cdecay_ref, x0_ref, xnext_ref, xp_ref, gpre_ref, win_ref, wout_ref,
                  gpost_ref, decay_ref, xi_ref, zeta_t_ref, bias_ref,
                  o_ref, proj_a, proj_b, h_a, h_b, state_ref, carry_ref, mixed_ref, *, tm,
                  tiles_per_seq):
    f32, bf16 = jnp.float32, jnp.bfloat16
    n = pl.program_id(0)
    seq_start = (jnp.maximum(n - 1, 0) % tiles_per_seq) == 0

    def normalise(x_ref, h_ref):
        x = x_ref[...]
        ms = jnp.mean(x * x, axis=-1, keepdims=True)
        h_ref[...] = (x * lax.rsqrt(ms + EPS) * gpre_ref[...]).astype(bf16)

    @pl.when(n == 0)
    def _():
        proj_b[...] = jnp.zeros_like(proj_b)
        normalise(x0_ref, h_a)

    @pl.when(seq_start)
    def _():
        state_ref[...] = jnp.zeros_like(state_ref)
        carry_ref[...] = jnp.zeros_like(carry_ref)

    row = lax.broadcasted_iota(jnp.int32, (CHUNK, CHUNK), 0)
    lane = lax.broadcasted_iota(jnp.int32, (CHUNK, CHUNK), 1)
    tri = lane <= row
    lo = lane < ATT_HEAD_DIM

    def project(h_ref, proj_w):
        for j in range(0, IN_COLS, PROJ_COLS):
            proj_w[:, j:j + PROJ_COLS] = _dot(h_ref[...], win_ref[:, j:j + PROJ_COLS])
            yield

    def mix(proj_r):
        prev = tuple(carry_ref[i] for i in range(N_CARRY))
        pairs, groups = range(RET_HEADS // 2), range(ATT_KV_HEADS)
        for c in range(tm // CHUNK):
            rows = slice(c * CHUNK, (c + 1) * CHUNK)

            def cols(base, i, width=CHUNK):
                return proj_r[rows, base + i * width:base + (i + 1) * width]

            scored = [_retention_pair_scores(cols(COL_RQ, hp, 2 * RET_DK), cols(COL_RK, hp, 2 * RET_DK))
                      for hp in pairs]
            cur = _attention_prep(cols(COL_AK, 0), cols(COL_AV, 0), lo)
            s_par = [_attention_scores(g, cols(COL_AQ, g, 2 * PAIR).astype(bf16), cur, prev)
                     for g in groups]
            yield
            ret = []
            for hd in range(RET_HEADS):
                sc = scored[hd // 2][0][:, (hd % 2) * CHUNK:(hd % 2 + 1) * CHUNK]
                ret.append(_retention_mix(sc, cols(COL_RQ, hd), cols(COL_RV, hd), state_ref[hd],
                                          decay_ref[hd], xi_ref[hd]))
            for hp in pairs:
                kv = _retention_pair_kv(scored[hp][1], cols(COL_RV, hp, 2 * RET_DV),
                                        zeta_t_ref[2 * hp], zeta_t_ref[2 * hp + 1])
                for e in range(2):
                    hd = 2 * hp + e
                    state_ref[hd] = cdecay_ref[hd] * state_ref[hd] + kv[:, e * RET_DV:(e + 1) * RET_DV]
            yield
            for hd in range(RET_HEADS):
                mixed_ref[rows, hd * RET_DV:(hd + 1) * RET_DV] = _retention_norm_gate(
                    ret[hd], cols(COL_RG, hd)).astype(bf16)
            first_idx = seq_start.astype(jnp.int32) if c == 0 else 0
            probs = [_attention_probs(g, s_par[g], bias_ref, first_idx, sinks_ref, tri, lo)
                     for g in groups]
            yield
            pv = [_attention_pv(g, probs[g][0], cur, prev, lo) for g in groups]
            yield
            for g in groups:
                for p in range(2):
                    i = g * 2 + p
                    o = pv[g][p * CHUNK:(p + 1) * CHUNK]
                    att = o[:, :PAIR] / (o[:, PAIR:] + probs[g][1][p])
                    mixed_ref[rows, RET_WIDTH + i * PAIR:RET_WIDTH + (i + 1) * PAIR] = (
                        att * _silu(cols(COL_AG, i))).astype(bf16)
            prev = cur
            yield
        for i in range(N_CARRY):
            carry_ref[i] = prev[i]

        out = _dot(mixed_ref[...], wout_ref[...])
        ms2 = jnp.mean(out * out, axis=-1, keepdims=True)
        o_ref[...] = xp_ref[...] + out * lax.rsqrt(ms2 + EPS) * gpost_ref[...]
        yield

    def step(h_cur, h_next, proj_w, proj_r):
        pieces = project(h_cur, proj_w)
        next(pieces)
        for i, _ in enumerate(mix(proj_r)):
            for _ in range(PIECES_PER_PHASE):
                next(pieces, None)
            if i == NORMALISE_AFTER_PHASE:
                normalise(xnext_ref, h_next)
        for _ in pieces:
            pass

    @pl.when(n % 2 == 0)
    def _():
        step(h_a, h_b, proj_a, proj_b)

    @pl.when(n % 2 == 1)
    def _():
        step(h_b, h_a, proj_b, proj_a)


def _retention_tables():
    f32 = jnp.float32
    c = CHUNK
    log_gamma = jnp.log1p(-jnp.exp2(-5.0 - jnp.arange(RET_HEADS, dtype=f32)))
    pos = jnp.arange(c, dtype=f32)
    diff = pos[:, None] - pos[None, :]
    intra = jnp.where(diff >= 0,
                      jnp.exp(log_gamma[:, None, None] * jnp.maximum(diff, 0.0)), 0.0)
    decay = intra * (RET_DK ** -0.5)
    xi = jnp.exp(log_gamma[:, None] * (pos + 1.0))
    zeta = jnp.exp(log_gamma[:, None] * (c - 1.0 - pos)) * (RET_DK ** -0.5)
    xi_b = jnp.broadcast_to(xi[:, :, None], (RET_HEADS, c, RET_DK))
    zeta_t = jnp.broadcast_to(zeta[:, None, :], (RET_HEADS, RET_DK, c))
    chunk_decay = jnp.exp(log_gamma * c)
    return decay, xi_b, zeta_t, chunk_decay


def _alibi_bias():
    f32 = jnp.float32
    t = CHUNK
    i = jnp.arange(t)[:, None]
    j = jnp.arange(t)[None, :]
    dist = jnp.where(j <= i, i - j, i + t - j).astype(f32)
    slopes = jnp.exp2(-8.0 * (jnp.arange(ATT_HEADS, dtype=f32) + 1.0) / ATT_HEADS)
    bias = LOG2E * slopes[:, None, None] * dist[None]
    first = jnp.where((j <= i)[None], bias, BIG)
    return jnp.stack([bias, first])


def kernel(x, g_pre, w_in, sinks, w_out, g_post):
    bsz, seq, d_model = x.shape
    assert d_model == D_MODEL and w_in.shape == (D_MODEL, IN_COLS)
    assert w_out.shape == (D_MIX, D_MODEL) and seq % TM == 0
    f32, bf16 = jnp.float32, jnp.bfloat16
    decay, xi_b, zeta_t, chunk_decay = _retention_tables()
    bias = _alibi_bias()
    n_tiles = bsz * seq // TM
    x2 = x.reshape(bsz * seq, D_MODEL)

    def const(shape):
        return pl.BlockSpec(shape, lambda n: (0,) * len(shape))

    smem = pl.BlockSpec(memory_space=pltpu.SMEM)
    out = pl.pallas_call(
        functools.partial(_layer_kernel, tm=TM, tiles_per_seq=seq // TM),
        grid=(n_tiles + 1,),
        in_specs=[
            smem,
            smem,
            pl.BlockSpec((TM, D_MODEL), lambda n: (0, 0)),
            pl.BlockSpec((TM, D_MODEL), lambda n: (jnp.minimum(n + 1, n_tiles - 1), 0)),
            pl.BlockSpec((TM, D_MODEL), lambda n: (jnp.maximum(n - 1, 0), 0)),
            const((1, D_MODEL)),
            const((D_MODEL, IN_COLS)),
            const((D_MIX, D_MODEL)),
            const((1, D_MODEL)),
            const((RET_HEADS, CHUNK, CHUNK)),
            const((RET_HEADS, CHUNK, RET_DK)),
            const((RET_HEADS, RET_DK, CHUNK)),
            const((2, ATT_HEADS, CHUNK, CHUNK)),
        ],
        out_specs=pl.BlockSpec((TM, D_MODEL), lambda n: (jnp.maximum(n - 1, 0), 0)),
        out_shape=jax.ShapeDtypeStruct(x2.shape, x.dtype),
        scratch_shapes=[
            pltpu.VMEM((TM, IN_COLS), f32),
            pltpu.VMEM((TM, IN_COLS), f32),
            pltpu.VMEM((TM, D_MODEL), bf16),
            pltpu.VMEM((TM, D_MODEL), bf16),
            pltpu.VMEM((RET_HEADS, RET_DK, RET_DV), f32),
            pltpu.VMEM((N_CARRY, CHUNK, CHUNK), bf16),
            pltpu.VMEM((TM, D_MIX), bf16),
        ],
        compiler_params=pltpu.CompilerParams(
            dimension_semantics=("arbitrary",),
            vmem_limit_bytes=VMEM_LIMIT_BYTES),
        name="hybrid_layer",
    )(sinks.astype(f32) * LOG2E, chunk_decay, x2, x2, x2, g_pre.reshape(1, D_MODEL).astype(f32),
      w_in.astype(bf16), w_out.astype(bf16), g_post.reshape(1, D_MODEL).astype(f32),
      decay, xi_b, zeta_t, bias)
    return out.reshape(x.shape)
```

```python
import functools

import jax
import jax.numpy as jnp
import numpy as np
from jax import lax
from jax.experimental import pallas as pl
from jax.experimental.pallas import tpu as pltpu

D_MODEL = 1024
RET_HEADS = 4
RET_DK = 128
RET_DV = 128
RET_WIDTH = RET_HEADS * RET_DV
CHUNK = 128
ATT_HEADS = 8
ATT_KV_HEADS = 2
ATT_GROUP = ATT_HEADS // ATT_KV_HEADS
ATT_HEAD_DIM = 64
ATT_WIDTH = ATT_HEADS * ATT_HEAD_DIM
ATT_KV_WIDTH = ATT_KV_HEADS * ATT_HEAD_DIM
D_MIX = RET_WIDTH + ATT_WIDTH
EPS = 1e-6
BIG = 1e30
LOG2E = 1.4426950408889634

COL_RQ = 0
COL_RK = COL_RQ + RET_HEADS * RET_DK
COL_RV = COL_RK + RET_HEADS * RET_DK
COL_RG = COL_RV + RET_WIDTH
COL_AQ = COL_RG + RET_WIDTH
COL_AK = COL_AQ + ATT_WIDTH
COL_AV = COL_AK + ATT_KV_WIDTH
COL_AG = COL_AV + ATT_KV_WIDTH
IN_COLS = COL_AG + ATT_WIDTH

PAIR = 2 * ATT_HEAD_DIM
assert PAIR == CHUNK and ATT_KV_WIDTH == CHUNK and RET_DK == CHUNK and RET_DV == CHUNK

TM = 512
VMEM_LIMIT_BYTES = 56 * 1024 * 1024
N_CARRY = 5
PROJ_COLS = 256
PIECES_PER_PHASE = 1
NORMALISE_AFTER_PHASE = 4
assert IN_COLS % PROJ_COLS == 0


def _silu(x):
    hx = 0.5 * x
    return hx + hx * jnp.tanh(hx)


_dot = functools.partial(jnp.dot, preferred_element_type=jnp.float32)


def _retention_pair_scores(q_pair, k_pair):
    bf16 = jnp.bfloat16
    k_ts = (k_pair[:, :RET_DK].T, k_pair[:, RET_DK:].T)
    zeros = jnp.zeros((RET_DK, CHUNK), bf16)
    rhs = jnp.concatenate([jnp.concatenate([k_ts[0].astype(bf16), zeros], axis=1),
                           jnp.concatenate([zeros, k_ts[1].astype(bf16)], axis=1)], axis=0)
    return _dot(q_pair.astype(bf16), rhs), k_ts


def _retention_mix(sc, q, v, st, decay, xi):
    bf16 = jnp.bfloat16
    lhs = jnp.concatenate([(sc * decay).astype(bf16), (q * xi).astype(bf16)], axis=1)
    rhs = jnp.concatenate([v.astype(bf16), st.astype(bf16)], axis=0)
    return _dot(lhs, rhs)


def _retention_pair_kv(k_ts, v_pair, zeta_a, zeta_b):
    bf16 = jnp.bfloat16
    lhs = jnp.concatenate([(k_ts[0] * zeta_a).astype(bf16), (k_ts[1] * zeta_b).astype(bf16)], axis=1)
    vb = v_pair.astype(bf16)
    zeros = jnp.zeros((CHUNK, RET_DV), bf16)
    rhs = jnp.concatenate([jnp.concatenate([vb[:, :RET_DV], zeros], axis=1),
                           jnp.concatenate([zeros, vb[:, RET_DV:]], axis=1)], axis=0)
    return _dot(lhs, rhs)


def _retention_norm_gate(o, gate):
    mu = jnp.mean(o, axis=-1, keepdims=True)
    d = o - mu
    var = jnp.mean(d * d, axis=-1, keepdims=True)
    return d * lax.rsqrt(var + EPS) * _silu(gate)


def _attention_prep(k_both, v_both, lo):
    bf16 = jnp.bfloat16
    k_t = (k_both.T * (ATT_HEAD_DIM ** -0.5 * LOG2E)).astype(bf16)
    v_roll = pltpu.roll(v_both, ATT_HEAD_DIM, axis=1)
    return (k_t,
            jnp.where(lo, v_both, 0.0).astype(bf16),
            jnp.where(lo, 0.0, v_roll).astype(bf16),
            jnp.where(lo, v_roll, 0.0).astype(bf16),
            jnp.where(lo, 0.0, v_both).astype(bf16))


def _attention_scores(g, q_g, cur, prev):
    hs = slice(g * ATT_HEAD_DIM, (g + 1) * ATT_HEAD_DIM)
    k_win = jnp.concatenate([cur[0][hs], prev[0][hs]], axis=1)
    zeros = jnp.zeros_like(k_win)
    rhs_even = jnp.concatenate([k_win, zeros], axis=0)
    rhs_odd = jnp.concatenate([zeros, k_win], axis=0)
    lhs = jnp.concatenate([q_g[:, :PAIR], q_g[:, PAIR:]], axis=0)
    return _dot(lhs, rhs_even), _dot(lhs, rhs_odd)


def _attention_probs(g, s_par, bias_ref, first_idx, sinks_ref, tri, lo):
    bf16 = jnp.bfloat16
    p_rows, inv_dens = [], []
    for p in range(2):
        parts, inv_e = [], []
        for e in range(2):
            hq = g * ATT_GROUP + p * 2 + e
            s = s_par[e][p * CHUNK:(p + 1) * CHUNK]
            sf = jnp.where(tri, s[:, :CHUNK], s[:, CHUNK:]) - bias_ref[first_idx, hq]
            m = jnp.max(sf, axis=-1, keepdims=True)
            ex = jnp.exp2(sf - m)
            inv_e.append(jnp.exp2(sinks_ref[hq] - m))
            parts += [jnp.where(tri, ex, 0.0).astype(bf16), jnp.where(tri, 0.0, ex).astype(bf16)]
        p_rows.append(jnp.concatenate(parts, axis=1))
        inv_dens.append(jnp.where(lo, inv_e[0], inv_e[1]))
    return jnp.concatenate(p_rows, axis=0), inv_dens


def _attention_pv(g, probs, cur, prev, lo):
    bf16 = jnp.bfloat16
    ones_l = jnp.where(lo, 1.0, 0.0).astype(bf16)
    ones_r = jnp.where(lo, 0.0, 1.0).astype(bf16)
    v_l, v_r = cur[1 + 2 * g], cur[2 + 2 * g]
    pv_l, pv_r = prev[1 + 2 * g], prev[2 + 2 * g]
    rhs = jnp.concatenate([jnp.concatenate([v_l, ones_l], axis=1),
                           jnp.concatenate([pv_l, ones_l], axis=1),
                           jnp.concatenate([v_r, ones_r], axis=1),
                           jnp.concatenate([pv_r, ones_r], axis=1)], axis=0)
    return _dot(probs, rhs)


def _layer_kernel(sinks_ref, cdecay_ref, xnext_ref, xp_ref, gpre_ref, win_ref, wout_ref,
                  gpost_ref, decay_ref, xi_ref, zeta_t_ref, bias_ref,
                  o_ref, proj_a, proj_b, h_a, h_b, state_ref, carry_ref, mixed_ref, *, tm,
                  tiles_per_seq, n_tiles):
    f32, bf16 = jnp.float32, jnp.bfloat16
    n = pl.program_id(0)
    first, last, even = n == 0, n == n_tiles, n % 2 == 0
    seq_start = (jnp.maximum(n - 1, 0) % tiles_per_seq) == 0

    def normalise(x_ref, h_ref):
        x = x_ref[...]
        ms = jnp.mean(x * x, axis=-1, keepdims=True)
        h_ref[...] = (x * lax.rsqrt(ms + EPS) * gpre_ref[...]).astype(bf16)

    @pl.when(first)
    def _():
        normalise(xp_ref, h_a)

    @pl.when(seq_start)
    def _():
        state_ref[...] = jnp.zeros_like(state_ref)
        carry_ref[...] = jnp.zeros_like(carry_ref)

    row = lax.broadcasted_iota(jnp.int32, (CHUNK, CHUNK), 0)
    lane = lax.broadcasted_iota(jnp.int32, (CHUNK, CHUNK), 1)
    tri = lane <= row
    lo = lane < ATT_HEAD_DIM

    def project(h_ref, proj_w):
        for j in range(0, IN_COLS, PROJ_COLS):
            proj_w[:, j:j + PROJ_COLS] = _dot(h_ref[...], win_ref[:, j:j + PROJ_COLS])
            yield

    def mix(proj_r):
        prev = tuple(carry_ref[i] for i in range(N_CARRY))
        pairs, groups = range(RET_HEADS // 2), range(ATT_KV_HEADS)
        for c in range(tm // CHUNK):
            rows = slice(c * CHUNK, (c + 1) * CHUNK)

            def cols(base, i, width=CHUNK):
                return proj_r[rows, base + i * width:base + (i + 1) * width]

            scored = [_retention_pair_scores(cols(COL_RQ, hp, 2 * RET_DK), cols(COL_RK, hp, 2 * RET_DK))
                      for hp in pairs]
            cur = _attention_prep(cols(COL_AK, 0), cols(COL_AV, 0), lo)
            s_par = [_attention_scores(g, cols(COL_AQ, g, 2 * PAIR).astype(bf16), cur, prev)
                     for g in groups]
            yield
            ret = []
            for hd in range(RET_HEADS):
                sc = scored[hd // 2][0][:, (hd % 2) * CHUNK:(hd % 2 + 1) * CHUNK]
                ret.append(_retention_mix(sc, cols(COL_RQ, hd), cols(COL_RV, hd), state_ref[hd],
                                          decay_ref[hd], xi_ref[hd]))
            for hp in pairs:
                kv = _retention_pair_kv(scored[hp][1], cols(COL_RV, hp, 2 * RET_DV),
                                        zeta_t_ref[2 * hp], zeta_t_ref[2 * hp + 1])
                for e in range(2):
                    hd = 2 * hp + e
                    state_ref[hd] = cdecay_ref[hd] * state_ref[hd] + kv[:, e * RET_DV:(e + 1) * RET_DV]
            yield
            for hd in range(RET_HEADS):
                mixed_ref[rows, hd * RET_DV:(hd + 1) * RET_DV] = _retention_norm_gate(
                    ret[hd], cols(COL_RG, hd)).astype(bf16)
            first_idx = seq_start.astype(jnp.int32) if c == 0 else 0
            probs = [_attention_probs(g, s_par[g], bias_ref, first_idx, sinks_ref, tri, lo)
                     for g in groups]
            yield
            pv = [_attention_pv(g, probs[g][0], cur, prev, lo) for g in groups]
            yield
            for g in groups:
                for p in range(2):
                    i = g * 2 + p
                    o = pv[g][p * CHUNK:(p + 1) * CHUNK]
                    att = o[:, :PAIR] / (o[:, PAIR:] + probs[g][1][p])
                    mixed_ref[rows, RET_WIDTH + i * PAIR:RET_WIDTH + (i + 1) * PAIR] = (
                        att * _silu(cols(COL_AG, i))).astype(bf16)
            prev = cur
            yield
        for i in range(N_CARRY):
            carry_ref[i] = prev[i]

        out = _dot(mixed_ref[...], wout_ref[...])
        ms2 = jnp.mean(out * out, axis=-1, keepdims=True)
        o_ref[...] = xp_ref[...] + out * lax.rsqrt(ms2 + EPS) * gpost_ref[...]
        yield

    def step(h_cur, h_next, proj_w, proj_r):
        pieces = project(h_cur, proj_w)
        next(pieces)
        for i, _ in enumerate(mix(proj_r)):
            for _ in range(PIECES_PER_PHASE):
                next(pieces, None)
            if i == NORMALISE_AFTER_PHASE:
                normalise(xnext_ref, h_next)
        for _ in pieces:
            pass

    @pl.when(first)
    def _():
        for _ in project(h_a, proj_a):
            pass
        normalise(xnext_ref, h_b)

    @pl.when(last)
    def _():
        for _ in mix(proj_b if n_tiles % 2 == 0 else proj_a):
            pass

    @pl.when(even & jnp.logical_not(first | last))
    def _():
        step(h_a, h_b, proj_a, proj_b)

    @pl.when(jnp.logical_not(even | last))
    def _():
        step(h_b, h_a, proj_b, proj_a)


def _retention_tables():
    f32 = np.float32
    c = CHUNK
    log_gamma = np.log1p(-np.exp2(-5.0 - np.arange(RET_HEADS, dtype=f32))).astype(f32)
    pos = np.arange(c, dtype=f32)
    diff = pos[:, None] - pos[None, :]
    intra = np.where(diff >= 0, np.exp(log_gamma[:, None, None] * np.maximum(diff, f32(0))), f32(0))
    decay = (intra * f32(RET_DK ** -0.5)).astype(f32)
    xi = np.exp(log_gamma[:, None] * (pos + f32(1)))
    zeta = np.exp(log_gamma[:, None] * (f32(c - 1) - pos)) * f32(RET_DK ** -0.5)
    xi_b = np.broadcast_to(xi[:, :, None], (RET_HEADS, c, RET_DK)).astype(f32)
    zeta_t = np.broadcast_to(zeta[:, None, :], (RET_HEADS, RET_DK, c)).astype(f32)
    chunk_decay = np.exp(log_gamma * f32(c)).astype(f32)
    return decay, xi_b, zeta_t, chunk_decay


def _alibi_bias():
    f32 = np.float32
    t = CHUNK
    i = np.arange(t)[:, None]
    j = np.arange(t)[None, :]
    dist = np.where(j <= i, i - j, i + t - j).astype(f32)
    slopes = np.exp2(-8.0 * (np.arange(ATT_HEADS, dtype=f32) + f32(1)) / f32(ATT_HEADS)).astype(f32)
    bias = (f32(LOG2E) * slopes[:, None, None] * dist[None]).astype(f32)
    first = np.where((j <= i)[None], bias, f32(BIG)).astype(f32)
    return np.stack([bias, first])


def kernel(x, g_pre, w_in, sinks, w_out, g_post):
    bsz, seq, d_model = x.shape
    assert d_model == D_MODEL and w_in.shape == (D_MODEL, IN_COLS)
    assert w_out.shape == (D_MIX, D_MODEL) and seq % TM == 0
    f32, bf16 = jnp.float32, jnp.bfloat16
    decay, xi_b, zeta_t, chunk_decay = _retention_tables()
    bias = _alibi_bias()
    n_tiles = bsz * seq // TM
    x2 = x.reshape(bsz * seq, D_MODEL)

    def const(shape):
        return pl.BlockSpec(shape, lambda n: (0,) * len(shape))

    smem = pl.BlockSpec(memory_space=pltpu.SMEM)
    out = pl.pallas_call(
        functools.partial(_layer_kernel, tm=TM, tiles_per_seq=seq // TM, n_tiles=n_tiles),
        grid=(n_tiles + 1,),
        in_specs=[
            smem,
            smem,
            pl.BlockSpec((TM, D_MODEL), lambda n: (jnp.minimum(n + 1, n_tiles - 1), 0)),
            pl.BlockSpec((TM, D_MODEL), lambda n: (jnp.maximum(n - 1, 0), 0)),
            const((1, D_MODEL)),
            const((D_MODEL, IN_COLS)),
            const((D_MIX, D_MODEL)),
            const((1, D_MODEL)),
            const((RET_HEADS, CHUNK, CHUNK)),
            const((RET_HEADS, CHUNK, RET_DK)),
            const((RET_HEADS, RET_DK, CHUNK)),
            const((2, ATT_HEADS, CHUNK, CHUNK)),
        ],
        out_specs=pl.BlockSpec((TM, D_MODEL), lambda n: (jnp.maximum(n - 1, 0), 0)),
        out_shape=jax.ShapeDtypeStruct(x2.shape, x.dtype),
        scratch_shapes=[
            pltpu.VMEM((TM, IN_COLS), f32),
            pltpu.VMEM((TM, IN_COLS), f32),
            pltpu.VMEM((TM, D_MODEL), bf16),
            pltpu.VMEM((TM, D_MODEL), bf16),
            pltpu.VMEM((RET_HEADS, RET_DK, RET_DV), f32),
            pltpu.VMEM((N_CARRY, CHUNK, CHUNK), bf16),
            pltpu.VMEM((TM, D_MIX), bf16),
        ],
        compiler_params=pltpu.CompilerParams(
            dimension_semantics=("arbitrary",),
            vmem_limit_bytes=VMEM_LIMIT_BYTES),
        name="hybrid_layer",
    )(sinks.astype(f32) * LOG2E, chunk_decay, x2, x2, g_pre.reshape(1, D_MODEL).astype(f32),
      w_in.astype(bf16), w_out.astype(bf16), g_post.reshape(1, D_MODEL).astype(f32),
      decay, xi_b, zeta_t, bias)
    return out.reshape(x.shape)
```

```python
import functools

import jax
import jax.numpy as jnp
import numpy as np
from jax import lax
from jax.experimental import pallas as pl
from jax.experimental.pallas import tpu as pltpu

D_MODEL = 1024
RET_HEADS = 4
RET_DK = 128
RET_DV = 128
RET_WIDTH = RET_HEADS * RET_DV
CHUNK = 128
ATT_HEADS = 8
ATT_KV_HEADS = 2
ATT_GROUP = ATT_HEADS // ATT_KV_HEADS
ATT_HEAD_DIM = 64
ATT_WIDTH = ATT_HEADS * ATT_HEAD_DIM
ATT_KV_WIDTH = ATT_KV_HEADS * ATT_HEAD_DIM
D_MIX = RET_WIDTH + ATT_WIDTH
EPS = 1e-6
BIG = 1e30
LOG2E = 1.4426950408889634

COL_RQ = 0
COL_RK = COL_RQ + RET_HEADS * RET_DK
COL_RV = COL_RK + RET_HEADS * RET_DK
COL_RG = COL_RV + RET_WIDTH
COL_AQ = COL_RG + RET_WIDTH
COL_AK = COL_AQ + ATT_WIDTH
COL_AV = COL_AK + ATT_KV_WIDTH
COL_AG = COL_AV + ATT_KV_WIDTH
IN_COLS = COL_AG + ATT_WIDTH

PAIR = 2 * ATT_HEAD_DIM
assert PAIR == CHUNK and ATT_KV_WIDTH == CHUNK and RET_DK == CHUNK and RET_DV == CHUNK

TM = 512
VMEM_LIMIT_BYTES = 56 * 1024 * 1024
N_CARRY = 5
PROJ_COLS = 256
N_STAGE = 8
OUT_ROW_PARTS = 1
PIECES_AFTER = {"start": 1, "softmax": 1, "gate": 1, "out": 4}
NORMALISE_AFTER_PHASE = 4
assert IN_COLS % PROJ_COLS == 0


def _silu(x):
    hx = 0.5 * x
    return hx + hx * jnp.tanh(hx)


_dot = functools.partial(jnp.dot, preferred_element_type=jnp.float32)


def _retention_pair_scores(q_pair, k_pair):
    bf16 = jnp.bfloat16
    k_ts = (k_pair[:, :RET_DK].T, k_pair[:, RET_DK:].T)
    zeros = jnp.zeros((RET_DK, CHUNK), bf16)
    rhs = jnp.concatenate([jnp.concatenate([k_ts[0].astype(bf16), zeros], axis=1),
                           jnp.concatenate([zeros, k_ts[1].astype(bf16)], axis=1)], axis=0)
    return _dot(q_pair.astype(bf16), rhs), k_ts


def _retention_mix(sc, q, v, st, decay, xi):
    bf16 = jnp.bfloat16
    lhs = jnp.concatenate([(sc * decay).astype(bf16), (q * xi).astype(bf16)], axis=1)
    rhs = jnp.concatenate([v.astype(bf16), st.astype(bf16)], axis=0)
    return _dot(lhs, rhs)


def _retention_pair_kv(k_ts, v_pair, zeta_a, zeta_b):
    bf16 = jnp.bfloat16
    lhs = jnp.concatenate([(k_ts[0] * zeta_a).astype(bf16), (k_ts[1] * zeta_b).astype(bf16)], axis=1)
    vb = v_pair.astype(bf16)
    zeros = jnp.zeros((CHUNK, RET_DV), bf16)
    rhs = jnp.concatenate([jnp.concatenate([vb[:, :RET_DV], zeros], axis=1),
                           jnp.concatenate([zeros, vb[:, RET_DV:]], axis=1)], axis=0)
    return _dot(lhs, rhs)


def _retention_norm_gate(o, gate):
    mu = jnp.mean(o, axis=-1, keepdims=True)
    d = o - mu
    var = jnp.mean(d * d, axis=-1, keepdims=True)
    return d * lax.rsqrt(var + EPS) * _silu(gate)


def _attention_prep(k_both, v_both, lo):
    bf16 = jnp.bfloat16
    k_t = (k_both.T * (ATT_HEAD_DIM ** -0.5 * LOG2E)).astype(bf16)
    v_roll = pltpu.roll(v_both, ATT_HEAD_DIM, axis=1)
    return (k_t,
            jnp.where(lo, v_both, 0.0).astype(bf16),
            jnp.where(lo, 0.0, v_roll).astype(bf16),
            jnp.where(lo, v_roll, 0.0).astype(bf16),
            jnp.where(lo, 0.0, v_both).astype(bf16))


def _attention_scores(g, q_g, cur, prev):
    hs = slice(g * ATT_HEAD_DIM, (g + 1) * ATT_HEAD_DIM)
    k_win = jnp.concatenate([cur[0][hs], prev[0][hs]], axis=1)
    zeros = jnp.zeros_like(k_win)
    rhs_even = jnp.concatenate([k_win, zeros], axis=0)
    rhs_odd = jnp.concatenate([zeros, k_win], axis=0)
    lhs = jnp.concatenate([q_g[:, :PAIR], q_g[:, PAIR:]], axis=0)
    return _dot(lhs, rhs_even), _dot(lhs, rhs_odd)


def _attention_probs(g, s_par, bias_ref, first_idx, sinks_ref, tri, lo):
    bf16 = jnp.bfloat16
    p_rows, inv_dens = [], []
    for p in range(2):
        parts, inv_e = [], []
        for e in range(2):
            hq = g * ATT_GROUP + p * 2 + e
            s = s_par[e][p * CHUNK:(p + 1) * CHUNK]
            sf = jnp.where(tri, s[:, :CHUNK], s[:, CHUNK:]) - bias_ref[first_idx, hq]
            m = jnp.max(sf, axis=-1, keepdims=True)
            ex = jnp.exp2(sf - m)
            inv_e.append(jnp.exp2(sinks_ref[hq] * LOG2E - m))
            parts += [jnp.where(tri, ex, 0.0).astype(bf16), jnp.where(tri, 0.0, ex).astype(bf16)]
        p_rows.append(jnp.concatenate(parts, axis=1))
        inv_dens.append(jnp.where(lo, inv_e[0], inv_e[1]))
    return jnp.concatenate(p_rows, axis=0), inv_dens


def _attention_pv(g, probs, cur, prev, lo):
    bf16 = jnp.bfloat16
    ones_l = jnp.where(lo, 1.0, 0.0).astype(bf16)
    ones_r = jnp.where(lo, 0.0, 1.0).astype(bf16)
    v_l, v_r = cur[1 + 2 * g], cur[2 + 2 * g]
    pv_l, pv_r = prev[1 + 2 * g], prev[2 + 2 * g]
    rhs = jnp.concatenate([jnp.concatenate([v_l, ones_l], axis=1),
                           jnp.concatenate([pv_l, ones_l], axis=1),
                           jnp.concatenate([v_r, ones_r], axis=1),
                           jnp.concatenate([pv_r, ones_r], axis=1)], axis=0)
    return _dot(probs, rhs)


def _layer_kernel(sinks_ref, cdecay_ref, xnext_ref, xp_ref, gpre_ref, win_hbm, wout_hbm,
                  gpost_ref, decay_ref, xi_ref, zeta_t_ref, bias_ref,
                  o_ref, proj_a, proj_b, h_a, h_b, state_ref, carry_ref, mixed_ref,
                  win_ref, wout_ref, stage_ref, stage_sems, *, tm, tiles_per_seq, n_tiles):
    f32, bf16 = jnp.float32, jnp.bfloat16
    n = pl.program_id(0)
    first, last, even = n == 0, n == n_tiles, n % 2 == 0
    seq_start = (jnp.maximum(n - 1, 0) % tiles_per_seq) == 0

    def normalise(x_ref, h_ref):
        x = x_ref[...]
        ms = jnp.mean(x * x, axis=-1, keepdims=True)
        h_ref[...] = (x * lax.rsqrt(ms + EPS) * gpre_ref[...]).astype(bf16)

    @pl.when(seq_start)
    def _():
        state_ref[...] = jnp.zeros_like(state_ref)
        carry_ref[...] = jnp.zeros_like(carry_ref)

    row = lax.broadcasted_iota(jnp.int32, (CHUNK, CHUNK), 0)
    lane = lax.broadcasted_iota(jnp.int32, (CHUNK, CHUNK), 1)
    tri = lane <= row
    lo = lane < ATT_HEAD_DIM

    def project(h_ref, proj_w, row_parts=1):
        rp = tm // row_parts
        for j in range(0, IN_COLS, PROJ_COLS):
            for r in range(0, tm, rp):
                proj_w[r:r + rp, j:j + PROJ_COLS] = _dot(h_ref[r:r + rp, :], win_ref[:, j:j + PROJ_COLS])
            yield

    def mix(proj_r):
        prev = tuple(carry_ref[i] for i in range(N_CARRY))
        pairs, groups = range(RET_HEADS // 2), range(ATT_KV_HEADS)
        for c in range(tm // CHUNK):
            rows = slice(c * CHUNK, (c + 1) * CHUNK)

            def cols(base, i, width=CHUNK):
                return proj_r[rows, base + i * width:base + (i + 1) * width]

            scored = [_retention_pair_scores(cols(COL_RQ, hp, 2 * RET_DK), cols(COL_RK, hp, 2 * RET_DK))
                      for hp in pairs]
            cur = _attention_prep(cols(COL_AK, 0), cols(COL_AV, 0), lo)
            s_par = [_attention_scores(g, cols(COL_AQ, g, 2 * PAIR).astype(bf16), cur, prev)
                     for g in groups]
            yield "scores"
            ret = []
            for hd in range(RET_HEADS):
                sc = scored[hd // 2][0][:, (hd % 2) * CHUNK:(hd % 2 + 1) * CHUNK]
                ret.append(_retention_mix(sc, cols(COL_RQ, hd), cols(COL_RV, hd), state_ref[hd],
                                          decay_ref[hd], xi_ref[hd]))
            for hp in pairs:
                kv = _retention_pair_kv(scored[hp][1], cols(COL_RV, hp, 2 * RET_DV),
                                        zeta_t_ref[2 * hp], zeta_t_ref[2 * hp + 1])
                for e in range(2):
                    hd = 2 * hp + e
                    state_ref[hd] = cdecay_ref[hd] * state_ref[hd] + kv[:, e * RET_DV:(e + 1) * RET_DV]
            yield "mix"
            for hd in range(RET_HEADS):
                mixed_ref[rows, hd * RET_DV:(hd + 1) * RET_DV] = _retention_norm_gate(
                    ret[hd], cols(COL_RG, hd)).astype(bf16)
            first_idx = seq_start.astype(jnp.int32) if c == 0 else 0
            probs = [_attention_probs(g, s_par[g], bias_ref, first_idx, sinks_ref, tri, lo)
                     for g in groups]
            yield "softmax"
            pv = [_attention_pv(g, probs[g][0], cur, prev, lo) for g in groups]
            yield "pv"
            for g in groups:
                for p in range(2):
                    i = g * 2 + p
                    o = pv[g][p * CHUNK:(p + 1) * CHUNK]
                    att = o[:, :PAIR] / (o[:, PAIR:] + probs[g][1][p])
                    mixed_ref[rows, RET_WIDTH + i * PAIR:RET_WIDTH + (i + 1) * PAIR] = (
                        att * _silu(cols(COL_AG, i))).astype(bf16)
            prev = cur
            yield "gate"
        for i in range(N_CARRY):
            carry_ref[i] = prev[i]

        rp = tm // OUT_ROW_PARTS
        for r in range(0, tm, rp):
            out = _dot(mixed_ref[r:r + rp, :], wout_ref[...])
            ms2 = jnp.mean(out * out, axis=-1, keepdims=True)
            o_ref[r:r + rp, :] = xp_ref[r:r + rp, :] + out * lax.rsqrt(ms2 + EPS) * gpost_ref[...]
            yield "out"

    def step(h_cur, h_next, proj_w, proj_r):
        pieces = project(h_cur, proj_w)
        for _ in range(PIECES_AFTER["start"]):
            next(pieces, None)
        for i, phase in enumerate(mix(proj_r)):
            for _ in range(PIECES_AFTER.get(phase, 0)):
                next(pieces, None)
            if i == NORMALISE_AFTER_PHASE:
                normalise(xnext_ref, h_next)
        for _ in pieces:
            pass

    w_blocks = ([(win_hbm, win_ref, j) for j in range(0, IN_COLS, PROJ_COLS)]
                + [(wout_hbm, wout_ref, j) for j in range(0, D_MODEL, PROJ_COLS)])

    def weight_copy(k):
        src, _, j = w_blocks[k]
        slot = k % N_STAGE
        return pltpu.make_async_copy(src.at[:, pl.ds(j, PROJ_COLS)], stage_ref.at[slot],
                                     stage_sems.at[slot])

    def first_step():
        for k in range(N_STAGE):
            weight_copy(k).start()
        normalise(xp_ref, h_a)
        pieces = project(h_a, proj_a, row_parts=2)
        for k, (_, dst, j) in enumerate(w_blocks):
            weight_copy(k).wait()
            dst[:, j:j + PROJ_COLS] = stage_ref[k % N_STAGE].astype(bf16)
            if k + N_STAGE < len(w_blocks):
                weight_copy(k + N_STAGE).start()
            if k >= 1:
                next(pieces, None)
        for _ in pieces:
            pass
        normalise(xnext_ref, h_b)

    pl.when(first)(first_step)

    @pl.when(last)
    def _():
        for _ in mix(proj_b if n_tiles % 2 == 0 else proj_a):
            pass

    @pl.when(even & jnp.logical_not(first | last))
    def _():
        step(h_a, h_b, proj_a, proj_b)

    @pl.when(jnp.logical_not(even | last))
    def _():
        step(h_b, h_a, proj_b, proj_a)


def _retention_tables():
    f32 = np.float32
    c = CHUNK
    log_gamma = np.log1p(-np.exp2(-5.0 - np.arange(RET_HEADS, dtype=f32))).astype(f32)
    pos = np.arange(c, dtype=f32)
    diff = pos[:, None] - pos[None, :]
    intra = np.where(diff >= 0, np.exp(log_gamma[:, None, None] * np.maximum(diff, f32(0))), f32(0))
    decay = (intra * f32(RET_DK ** -0.5)).astype(f32)
    xi = np.exp(log_gamma[:, None] * (pos + f32(1)))
    zeta = np.exp(log_gamma[:, None] * (f32(c - 1) - pos)) * f32(RET_DK ** -0.5)
    xi_b = np.broadcast_to(xi[:, :, None], (RET_HEADS, c, RET_DK)).astype(f32)
    zeta_t = np.broadcast_to(zeta[:, None, :], (RET_HEADS, RET_DK, c)).astype(f32)
    chunk_decay = np.exp(log_gamma * f32(c)).astype(f32)
    return decay, xi_b, zeta_t, chunk_decay


def _alibi_bias():
    f32 = np.float32
    t = CHUNK
    i = np.arange(t)[:, None]
    j = np.arange(t)[None, :]
    dist = np.where(j <= i, i - j, i + t - j).astype(f32)
    slopes = np.exp2(-8.0 * (np.arange(ATT_HEADS, dtype=f32) + f32(1)) / f32(ATT_HEADS)).astype(f32)
    bias = (f32(LOG2E) * slopes[:, None, None] * dist[None]).astype(f32)
    first = np.where((j <= i)[None], bias, f32(BIG)).astype(f32)
    return np.stack([bias, first])


def kernel(x, g_pre, w_in, sinks, w_out, g_post):
    bsz, seq, d_model = x.shape
    assert d_model == D_MODEL and w_in.shape == (D_MODEL, IN_COLS)
    assert w_out.shape == (D_MIX, D_MODEL) and D_MIX == D_MODEL and seq % TM == 0
    f32, bf16 = jnp.float32, jnp.bfloat16
    decay, xi_b, zeta_t, chunk_decay = _retention_tables()
    bias = _alibi_bias()
    n_tiles = bsz * seq // TM
    x2 = x.reshape(bsz * seq, D_MODEL)

    def const(shape):
        return pl.BlockSpec(shape, lambda n: (0,) * len(shape))

    smem = pl.BlockSpec(memory_space=pltpu.SMEM)
    out = pl.pallas_call(
        functools.partial(_layer_kernel, tm=TM, tiles_per_seq=seq // TM, n_tiles=n_tiles),
        grid=(n_tiles + 1,),
        in_specs=[
            smem,
            smem,
            pl.BlockSpec((TM, D_MODEL), lambda n: (jnp.minimum(n + 1, n_tiles - 1), 0)),
            pl.BlockSpec((TM, D_MODEL), lambda n: (jnp.maximum(n - 1, 0), 0)),
            const((1, D_MODEL)),
            pl.BlockSpec(memory_space=pl.ANY),
            pl.BlockSpec(memory_space=pl.ANY),
            const((1, D_MODEL)),
            const((RET_HEADS, CHUNK, CHUNK)),
            const((RET_HEADS, CHUNK, RET_DK)),
            const((RET_HEADS, RET_DK, CHUNK)),
            const((2, ATT_HEADS, CHUNK, CHUNK)),
        ],
        out_specs=pl.BlockSpec((TM, D_MODEL), lambda n: (jnp.maximum(n - 1, 0), 0)),
        out_shape=jax.ShapeDtypeStruct(x2.shape, x.dtype),
        scratch_shapes=[
            pltpu.VMEM((TM, IN_COLS), f32),
            pltpu.VMEM((TM, IN_COLS), f32),
            pltpu.VMEM((TM, D_MODEL), bf16),
            pltpu.VMEM((TM, D_MODEL), bf16),
            pltpu.VMEM((RET_HEADS, RET_DK, RET_DV), f32),
            pltpu.VMEM((N_CARRY, CHUNK, CHUNK), bf16),
            pltpu.VMEM((TM, D_MIX), bf16),
            pltpu.VMEM((D_MODEL, IN_COLS), bf16),
            pltpu.VMEM((D_MIX, D_MODEL), bf16),
            pltpu.VMEM((N_STAGE, D_MODEL, PROJ_COLS), f32),
            pltpu.SemaphoreType.DMA((N_STAGE,)),
        ],
        compiler_params=pltpu.CompilerParams(
            dimension_semantics=("arbitrary",),
            vmem_limit_bytes=VMEM_LIMIT_BYTES),
        name="hybrid_layer",
    )(sinks.astype(f32), chunk_decay, x2, x2, g_pre.reshape(1, D_MODEL).astype(f32),
      w_in.astype(f32), w_out.astype(f32), g_post.reshape(1, D_MODEL).astype(f32),
      decay, xi_b, zeta_t, bias)
    return out.reshape(x.shape)
```

```python
import functools

import jax
import jax.numpy as jnp
import numpy as np
from jax import lax
from jax.experimental import pallas as pl
from jax.experimental.pallas import tpu as pltpu

D_MODEL = 1024
RET_HEADS = 4
RET_DK = 128
RET_DV = 128
RET_WIDTH = RET_HEADS * RET_DV
CHUNK = 128
ATT_HEADS = 8
ATT_KV_HEADS = 2
ATT_GROUP = ATT_HEADS // ATT_KV_HEADS
ATT_HEAD_DIM = 64
ATT_WIDTH = ATT_HEADS * ATT_HEAD_DIM
ATT_KV_WIDTH = ATT_KV_HEADS * ATT_HEAD_DIM
D_MIX = RET_WIDTH + ATT_WIDTH
EPS = 1e-6
BIG = 1e30
LOG2E = 1.4426950408889634

COL_RQ = 0
COL_RK = COL_RQ + RET_HEADS * RET_DK
COL_RV = COL_RK + RET_HEADS * RET_DK
COL_RG = COL_RV + RET_WIDTH
COL_AQ = COL_RG + RET_WIDTH
COL_AK = COL_AQ + ATT_WIDTH
COL_AV = COL_AK + ATT_KV_WIDTH
COL_AG = COL_AV + ATT_KV_WIDTH
IN_COLS = COL_AG + ATT_WIDTH

PAIR = 2 * ATT_HEAD_DIM
assert PAIR == CHUNK and ATT_KV_WIDTH == CHUNK and RET_DK == CHUNK and RET_DV == CHUNK

TM = 512
VMEM_LIMIT_BYTES = 56 * 1024 * 1024
N_CARRY = 5
PROJ_COLS = 256
N_STAGE = 8
W_GROUP = 4
assert N_STAGE % W_GROUP == 0
OUT_ROW_PARTS = 2
PIECES_AFTER = {"start": 1, "softmax": 1, "gate": 1, "out": 2}
NORMALISE_AFTER_PHASE = 4
assert IN_COLS % PROJ_COLS == 0


def _silu(x):
    hx = 0.5 * x
    return hx + hx * jnp.tanh(hx)


_dot = functools.partial(jnp.dot, preferred_element_type=jnp.float32)


def _retention_pair_scores(q_pair, k_pair):
    bf16 = jnp.bfloat16
    k_ts = (k_pair[:, :RET_DK].T, k_pair[:, RET_DK:].T)
    zeros = jnp.zeros((RET_DK, CHUNK), bf16)
    rhs = jnp.concatenate([jnp.concatenate([k_ts[0].astype(bf16), zeros], axis=1),
                           jnp.concatenate([zeros, k_ts[1].astype(bf16)], axis=1)], axis=0)
    return _dot(q_pair.astype(bf16), rhs), k_ts


def _retention_mix(sc, q, v, st, decay, xi):
    bf16 = jnp.bfloat16
    lhs = jnp.concatenate([(sc * decay).astype(bf16), (q * xi).astype(bf16)], axis=1)
    rhs = jnp.concatenate([v.astype(bf16), st.astype(bf16)], axis=0)
    return _dot(lhs, rhs)


def _retention_pair_kv(k_ts, v_pair, zeta_a, zeta_b):
    bf16 = jnp.bfloat16
    lhs = jnp.concatenate([(k_ts[0] * zeta_a).astype(bf16), (k_ts[1] * zeta_b).astype(bf16)], axis=1)
    vb = v_pair.astype(bf16)
    zeros = jnp.zeros((CHUNK, RET_DV), bf16)
    rhs = jnp.concatenate([jnp.concatenate([vb[:, :RET_DV], zeros], axis=1),
                           jnp.concatenate([zeros, vb[:, RET_DV:]], axis=1)], axis=0)
    return _dot(lhs, rhs)


def _retention_norm_gate(o, gate):
    mu = jnp.mean(o, axis=-1, keepdims=True)
    d = o - mu
    var = jnp.mean(d * d, axis=-1, keepdims=True)
    return d * lax.rsqrt(var + EPS) * _silu(gate)


def _attention_prep(k_both, v_both, lo):
    bf16 = jnp.bfloat16
    k_t = (k_both.T * (ATT_HEAD_DIM ** -0.5 * LOG2E)).astype(bf16)
    v_roll = pltpu.roll(v_both, ATT_HEAD_DIM, axis=1)
    return (k_t,
            jnp.where(lo, v_both, 0.0).astype(bf16),
            jnp.where(lo, 0.0, v_roll).astype(bf16),
            jnp.where(lo, v_roll, 0.0).astype(bf16),
            jnp.where(lo, 0.0, v_both).astype(bf16))


def _attention_scores(g, q_g, cur, prev):
    hs = slice(g * ATT_HEAD_DIM, (g + 1) * ATT_HEAD_DIM)
    k_win = jnp.concatenate([cur[0][hs], prev[0][hs]], axis=1)
    zeros = jnp.zeros_like(k_win)
    rhs_even = jnp.concatenate([k_win, zeros], axis=0)
    rhs_odd = jnp.concatenate([zeros, k_win], axis=0)
    lhs = jnp.concatenate([q_g[:, :PAIR], q_g[:, PAIR:]], axis=0)
    return _dot(lhs, rhs_even), _dot(lhs, rhs_odd)


def _attention_probs(g, s_par, bias_ref, first_idx, sinks_ref, tri, lo):
    bf16 = jnp.bfloat16
    p_rows, inv_dens = [], []
    for p in range(2):
        parts, inv_e = [], []
        for e in range(2):
            hq = g * ATT_GROUP + p * 2 + e
            s = s_par[e][p * CHUNK:(p + 1) * CHUNK]
            sf = jnp.where(tri, s[:, :CHUNK], s[:, CHUNK:]) - bias_ref[first_idx, hq]
            m = jnp.max(sf, axis=-1, keepdims=True)
            ex = jnp.exp2(sf - m)
            inv_e.append(jnp.exp2(sinks_ref[hq] * LOG2E - m))
            parts += [jnp.where(tri, ex, 0.0).astype(bf16), jnp.where(tri, 0.0, ex).astype(bf16)]
        p_rows.append(jnp.concatenate(parts, axis=1))
        inv_dens.append(jnp.where(lo, inv_e[0], inv_e[1]))
    return jnp.concatenate(p_rows, axis=0), inv_dens


def _attention_pv(g, probs, cur, prev, lo):
    bf16 = jnp.bfloat16
    ones_l = jnp.where(lo, 1.0, 0.0).astype(bf16)
    ones_r = jnp.where(lo, 0.0, 1.0).astype(bf16)
    v_l, v_r = cur[1 + 2 * g], cur[2 + 2 * g]
    pv_l, pv_r = prev[1 + 2 * g], prev[2 + 2 * g]
    rhs = jnp.concatenate([jnp.concatenate([v_l, ones_l], axis=1),
                           jnp.concatenate([pv_l, ones_l], axis=1),
                           jnp.concatenate([v_r, ones_r], axis=1),
                           jnp.concatenate([pv_r, ones_r], axis=1)], axis=0)
    return _dot(probs, rhs)


def _layer_kernel(sinks_ref, cdecay_ref, xnext_ref, xp_ref, gpre_ref, win_hbm, wout_hbm,
                  gpost_ref, decay_ref, xi_ref, zeta_t_ref, bias_ref,
                  o_ref, proj_a, proj_b, h_a, h_b, state_ref, carry_ref, mixed_ref,
                  win_ref, wout_ref, stage_ref, stage_sems, *, tm, tiles_per_seq, n_tiles):
    f32, bf16 = jnp.float32, jnp.bfloat16
    n = pl.program_id(0)
    first, last, even = n == 0, n == n_tiles, n % 2 == 0
    seq_start = (jnp.maximum(n - 1, 0) % tiles_per_seq) == 0

    def normalise(x_ref, h_ref):
        x = x_ref[...]
        ms = jnp.mean(x * x, axis=-1, keepdims=True)
        h_ref[...] = (x * lax.rsqrt(ms + EPS) * gpre_ref[...]).astype(bf16)

    @pl.when(seq_start)
    def _():
        state_ref[...] = jnp.zeros_like(state_ref)
        carry_ref[...] = jnp.zeros_like(carry_ref)

    row = lax.broadcasted_iota(jnp.int32, (CHUNK, CHUNK), 0)
    lane = lax.broadcasted_iota(jnp.int32, (CHUNK, CHUNK), 1)
    tri = lane <= row
    lo = lane < ATT_HEAD_DIM

    def project(h_ref, proj_w, row_parts=1):
        rp = tm // row_parts
        for j in range(0, IN_COLS, PROJ_COLS):
            for r in range(0, tm, rp):
                proj_w[r:r + rp, j:j + PROJ_COLS] = _dot(h_ref[r:r + rp, :], win_ref[:, j:j + PROJ_COLS])
            yield

    def mix(proj_r):
        prev = tuple(carry_ref[i] for i in range(N_CARRY))
        pairs, groups = range(RET_HEADS // 2), range(ATT_KV_HEADS)
        for c in range(tm // CHUNK):
            rows = slice(c * CHUNK, (c + 1) * CHUNK)

            def cols(base, i, width=CHUNK):
                return proj_r[rows, base + i * width:base + (i + 1) * width]

            scored = [_retention_pair_scores(cols(COL_RQ, hp, 2 * RET_DK), cols(COL_RK, hp, 2 * RET_DK))
                      for hp in pairs]
            cur = _attention_prep(cols(COL_AK, 0), cols(COL_AV, 0), lo)
            s_par = [_attention_scores(g, cols(COL_AQ, g, 2 * PAIR).astype(bf16), cur, prev)
                     for g in groups]
            yield "scores"
            ret = []
            for hd in range(RET_HEADS):
                sc = scored[hd // 2][0][:, (hd % 2) * CHUNK:(hd % 2 + 1) * CHUNK]
                ret.append(_retention_mix(sc, cols(COL_RQ, hd), cols(COL_RV, hd), state_ref[hd],
                                          decay_ref[hd], xi_ref[hd]))
            for hp in pairs:
                kv = _retention_pair_kv(scored[hp][1], cols(COL_RV, hp, 2 * RET_DV),
                                        zeta_t_ref[2 * hp], zeta_t_ref[2 * hp + 1])
                for e in range(2):
                    hd = 2 * hp + e
                    state_ref[hd] = cdecay_ref[hd] * state_ref[hd] + kv[:, e * RET_DV:(e + 1) * RET_DV]
            yield "mix"
            for hd in range(RET_HEADS):
                mixed_ref[rows, hd * RET_DV:(hd + 1) * RET_DV] = _retention_norm_gate(
                    ret[hd], cols(COL_RG, hd)).astype(bf16)
            first_idx = seq_start.astype(jnp.int32) if c == 0 else 0
            probs = [_attention_probs(g, s_par[g], bias_ref, first_idx, sinks_ref, tri, lo)
                     for g in groups]
            yield "softmax"
            pv = [_attention_pv(g, probs[g][0], cur, prev, lo) for g in groups]
            yield "pv"
            for g in groups:
                for p in range(2):
                    i = g * 2 + p
                    o = pv[g][p * CHUNK:(p + 1) * CHUNK]
                    att = o[:, :PAIR] / (o[:, PAIR:] + probs[g][1][p])
                    mixed_ref[rows, RET_WIDTH + i * PAIR:RET_WIDTH + (i + 1) * PAIR] = (
                        att * _silu(cols(COL_AG, i))).astype(bf16)
            prev = cur
            yield "gate"
        for i in range(N_CARRY):
            carry_ref[i] = prev[i]

        rp = tm // OUT_ROW_PARTS
        for r in range(0, tm, rp):
            out = _dot(mixed_ref[r:r + rp, :], wout_ref[...])
            ms2 = jnp.mean(out * out, axis=-1, keepdims=True)
            o_ref[r:r + rp, :] = xp_ref[r:r + rp, :] + out * lax.rsqrt(ms2 + EPS) * gpost_ref[...]
            yield "out"

    def step(h_cur, h_next, proj_w, proj_r):
        pieces = project(h_cur, proj_w)
        for _ in range(PIECES_AFTER["start"]):
            next(pieces, None)
        for i, phase in enumerate(mix(proj_r)):
            for _ in range(PIECES_AFTER.get(phase, 0)):
                next(pieces, None)
            if i == NORMALISE_AFTER_PHASE:
                normalise(xnext_ref, h_next)
        for _ in pieces:
            pass

    w_blocks = ([(win_hbm, win_ref, j) for j in range(0, IN_COLS, PROJ_COLS)]
                + [(wout_hbm, wout_ref, j) for j in range(0, D_MODEL, PROJ_COLS)])

    def weight_copy(k):
        src, _, j = w_blocks[k]
        slot = k % N_STAGE
        return pltpu.make_async_copy(src.at[:, pl.ds(j, PROJ_COLS)], stage_ref.at[slot],
                                     stage_sems.at[slot])

    def first_step():
        nb = len(w_blocks)
        for k in range(N_STAGE):
            weight_copy(k).start()
        normalise(xp_ref, h_a)
        pieces = project(h_a, proj_a)
        for g0 in range(0, nb, W_GROUP):
            group = range(g0, min(g0 + W_GROUP, nb))
            for k in group:
                weight_copy(k).wait()
            for k in group:
                _, dst, j = w_blocks[k]
                dst[:, j:j + PROJ_COLS] = stage_ref[k % N_STAGE].astype(bf16)
            for k in group:
                if k + N_STAGE < nb:
                    weight_copy(k + N_STAGE).start()
            if g0 > 0:
                for _ in range(W_GROUP):
                    next(pieces, None)
        for _ in pieces:
            pass
        normalise(xnext_ref, h_b)

    pl.when(first)(first_step)

    @pl.when(last)
    def _():
        for _ in mix(proj_b if n_tiles % 2 == 0 else proj_a):
            pass

    @pl.when(even & jnp.logical_not(first | last))
    def _():
        step(h_a, h_b, proj_a, proj_b)

    @pl.when(jnp.logical_not(even | last))
    def _():
        step(h_b, h_a, proj_b, proj_a)


def _retention_tables():
    f32 = np.float32
    c = CHUNK
    log_gamma = np.log1p(-np.exp2(-5.0 - np.arange(RET_HEADS, dtype=f32))).astype(f32)
    pos = np.arange(c, dtype=f32)
    diff = pos[:, None] - pos[None, :]
    intra = np.where(diff >= 0, np.exp(log_gamma[:, None, None] * np.maximum(diff, f32(0))), f32(0))
    decay = (intra * f32(RET_DK ** -0.5)).astype(f32)
    xi = np.exp(log_gamma[:, None] * (pos + f32(1)))
    zeta = np.exp(log_gamma[:, None] * (f32(c - 1) - pos)) * f32(RET_DK ** -0.5)
    xi_b = np.broadcast_to(xi[:, :, None], (RET_HEADS, c, RET_DK)).astype(f32)
    zeta_t = np.broadcast_to(zeta[:, None, :], (RET_HEADS, RET_DK, c)).astype(f32)
    chunk_decay = np.exp(log_gamma * f32(c)).astype(f32)
    return decay, xi_b, zeta_t, chunk_decay


def _alibi_bias():
    f32 = np.float32
    t = CHUNK
    i = np.arange(t)[:, None]
    j = np.arange(t)[None, :]
    dist = np.where(j <= i, i - j, i + t - j).astype(f32)
    slopes = np.exp2(-8.0 * (np.arange(ATT_HEADS, dtype=f32) + f32(1)) / f32(ATT_HEADS)).astype(f32)
    bias = (f32(LOG2E) * slopes[:, None, None] * dist[None]).astype(f32)
    first = np.where((j <= i)[None], bias, f32(BIG)).astype(f32)
    return np.stack([bias, first])


def kernel(x, g_pre, w_in, sinks, w_out, g_post):
    bsz, seq, d_model = x.shape
    assert d_model == D_MODEL and w_in.shape == (D_MODEL, IN_COLS)
    assert w_out.shape == (D_MIX, D_MODEL) and D_MIX == D_MODEL and seq % TM == 0
    f32, bf16 = jnp.float32, jnp.bfloat16
    decay, xi_b, zeta_t, chunk_decay = _retention_tables()
    bias = _alibi_bias()
    n_tiles = bsz * seq // TM
    x2 = x.reshape(bsz * seq, D_MODEL)

    def const(shape):
        return pl.BlockSpec(shape, lambda n: (0,) * len(shape))

    smem = pl.BlockSpec(memory_space=pltpu.SMEM)
    out = pl.pallas_call(
        functools.partial(_layer_kernel, tm=TM, tiles_per_seq=seq // TM, n_tiles=n_tiles),
        grid=(n_tiles + 1,),
        in_specs=[
            smem,
            smem,
            pl.BlockSpec((TM, D_MODEL), lambda n: (jnp.minimum(n + 1, n_tiles - 1), 0)),
            pl.BlockSpec((TM, D_MODEL), lambda n: (jnp.maximum(n - 1, 0), 0)),
            const((1, D_MODEL)),
            pl.BlockSpec(memory_space=pl.ANY),
            pl.BlockSpec(memory_space=pl.ANY),
            const((1, D_MODEL)),
            const((RET_HEADS, CHUNK, CHUNK)),
            const((RET_HEADS, CHUNK, RET_DK)),
            const((RET_HEADS, RET_DK, CHUNK)),
            const((2, ATT_HEADS, CHUNK, CHUNK)),
        ],
        out_specs=pl.BlockSpec((TM, D_MODEL), lambda n: (jnp.maximum(n - 1, 0), 0)),
        out_shape=jax.ShapeDtypeStruct(x2.shape, x.dtype),
        scratch_shapes=[
            pltpu.VMEM((TM, IN_COLS), f32),
            pltpu.VMEM((TM, IN_COLS), f32),
            pltpu.VMEM((TM, D_MODEL), bf16),
            pltpu.VMEM((TM, D_MODEL), bf16),
            pltpu.VMEM((RET_HEADS, RET_DK, RET_DV), f32),
            pltpu.VMEM((N_CARRY, CHUNK, CHUNK), bf16),
            pltpu.VMEM((TM, D_MIX), bf16),
            pltpu.VMEM((D_MODEL, IN_COLS), bf16),
            pltpu.VMEM((D_MIX, D_MODEL), bf16),
            pltpu.VMEM((N_STAGE, D_MODEL, PROJ_COLS), f32),
            pltpu.SemaphoreType.DMA((N_STAGE,)),
        ],
        compiler_params=pltpu.CompilerParams(
            dimension_semantics=("arbitrary",),
            vmem_limit_bytes=VMEM_LIMIT_BYTES),
        name="hybrid_layer",
    )(sinks.astype(f32), chunk_decay, x2, x2, g_pre.reshape(1, D_MODEL).astype(f32),
      w_in.astype(f32), w_out.astype(f32), g_post.reshape(1, D_MODEL).astype(f32),
      decay, xi_b, zeta_t, bias)
    return out.reshape(x.shape)
```

```python
import functools

import jax
import jax.numpy as jnp
import numpy as np
from jax import lax
from jax.experimental import pallas as pl
from jax.experimental.pallas import tpu as pltpu

D_MODEL = 1024
RET_HEADS = 4
RET_DK = 128
RET_DV = 128
RET_WIDTH = RET_HEADS * RET_DV
CHUNK = 128
ATT_HEADS = 8
ATT_KV_HEADS = 2
ATT_GROUP = ATT_HEADS // ATT_KV_HEADS
ATT_HEAD_DIM = 64
ATT_WIDTH = ATT_HEADS * ATT_HEAD_DIM
ATT_KV_WIDTH = ATT_KV_HEADS * ATT_HEAD_DIM
D_MIX = RET_WIDTH + ATT_WIDTH
EPS = 1e-6
BIG = 1e30
LOG2E = 1.4426950408889634

COL_RQ = 0
COL_RK = COL_RQ + RET_HEADS * RET_DK
COL_RV = COL_RK + RET_HEADS * RET_DK
COL_RG = COL_RV + RET_WIDTH
COL_AQ = COL_RG + RET_WIDTH
COL_AK = COL_AQ + ATT_WIDTH
COL_AV = COL_AK + ATT_KV_WIDTH
COL_AG = COL_AV + ATT_KV_WIDTH
IN_COLS = COL_AG + ATT_WIDTH

PAIR = 2 * ATT_HEAD_DIM
assert PAIR == CHUNK and ATT_KV_WIDTH == CHUNK and RET_DK == CHUNK and RET_DV == CHUNK

TM = 512
VMEM_LIMIT_BYTES = 56 * 1024 * 1024
N_CARRY = 2
PROJ_COLS = 256
N_STAGE = 8
W_GROUP = 4
assert N_STAGE % W_GROUP == 0
OUT_ROW_PARTS = 2
PIECES_AFTER = {"start": 1, "softmax": 1, "gate": 1, "out": 2}
NORMALISE_AFTER_PHASE = 4
assert IN_COLS % PROJ_COLS == 0


def _silu(x):
    hx = 0.5 * x
    return hx + hx * jnp.tanh(hx)


_dot = functools.partial(jnp.dot, preferred_element_type=jnp.float32)


def _retention_pair_scores(q_pair, k_pair):
    bf16 = jnp.bfloat16
    k_ts = (k_pair[:, :RET_DK].T, k_pair[:, RET_DK:].T)
    zeros = jnp.zeros((RET_DK, CHUNK), bf16)
    rhs = jnp.concatenate([jnp.concatenate([k_ts[0].astype(bf16), zeros], axis=1),
                           jnp.concatenate([zeros, k_ts[1].astype(bf16)], axis=1)], axis=0)
    return _dot(q_pair.astype(bf16), rhs), k_ts


def _retention_mix(sc, q, v, st, decay, xi):
    bf16 = jnp.bfloat16
    lhs = jnp.concatenate([(sc * decay).astype(bf16), (q * xi).astype(bf16)], axis=1)
    rhs = jnp.concatenate([v.astype(bf16), st.astype(bf16)], axis=0)
    return _dot(lhs, rhs)


def _retention_pair_kv(k_ts, v_pair, zeta_a, zeta_b):
    bf16 = jnp.bfloat16
    lhs = jnp.concatenate([(k_ts[0] * zeta_a).astype(bf16), (k_ts[1] * zeta_b).astype(bf16)], axis=1)
    vb = v_pair.astype(bf16)
    zeros = jnp.zeros((CHUNK, RET_DV), bf16)
    rhs = jnp.concatenate([jnp.concatenate([vb[:, :RET_DV], zeros], axis=1),
                           jnp.concatenate([zeros, vb[:, RET_DV:]], axis=1)], axis=0)
    return _dot(lhs, rhs)


def _retention_norm_gate(o, gate):
    mu = jnp.mean(o, axis=-1, keepdims=True)
    d = o - mu
    var = jnp.mean(d * d, axis=-1, keepdims=True)
    return d * lax.rsqrt(var + EPS) * _silu(gate)


def _attention_prep(k_both, v_both):
    bf16 = jnp.bfloat16
    return (k_both * (ATT_HEAD_DIM ** -0.5 * LOG2E)).astype(bf16), v_both.T.astype(bf16)


def _attention_scores(g, q_g, cur, prev):
    bf16 = jnp.bfloat16
    k_win = jnp.concatenate([cur[0], prev[0]], axis=0)
    zeros = jnp.zeros((ATT_HEAD_DIM, CHUNK), bf16)
    cols = []
    for p in range(2):
        q_t = q_g[:, p * PAIR:(p + 1) * PAIR].T.astype(bf16)
        for e in range(2):
            q_h = q_t[e * ATT_HEAD_DIM:(e + 1) * ATT_HEAD_DIM]
            cols.append(jnp.concatenate([q_h, zeros] if g == 0 else [zeros, q_h], axis=0))
    return _dot(k_win, jnp.concatenate(cols, axis=1))


def _attention_probs(g, s_t, bias_t_ref, first_idx, sinks_ref, tri_t):
    bf16 = jnp.bfloat16
    parts, inv_dens = [], []
    for hh in range(ATT_GROUP):
        hq = g * ATT_GROUP + hh
        s = s_t[:, hh * CHUNK:(hh + 1) * CHUNK]
        sf = jnp.where(tri_t, s[:CHUNK], s[CHUNK:]) - bias_t_ref[first_idx, hq]
        m = jnp.max(sf, axis=0, keepdims=True)
        ex = jnp.exp2(sf - m)
        den = jnp.sum(ex, axis=0, keepdims=True) + jnp.exp2(sinks_ref[hq] * LOG2E - m)
        inv_dens.append(1.0 / den)
        parts.append(jnp.concatenate([jnp.where(tri_t, ex, 0.0).astype(bf16),
                                      jnp.where(tri_t, 0.0, ex).astype(bf16)], axis=0))
    return jnp.concatenate(parts, axis=1), inv_dens


def _attention_pv(g, probs_t, cur, prev):
    hs = slice(g * ATT_HEAD_DIM, (g + 1) * ATT_HEAD_DIM)
    return _dot(jnp.concatenate([cur[1][hs], prev[1][hs]], axis=1), probs_t)


def _layer_kernel(sinks_ref, cdecay_ref, xnext_ref, xp_ref, gpre_ref, win_hbm, wout_hbm,
                  gpost_ref, decay_ref, xi_ref, zeta_t_ref, bias_ref,
                  o_ref, proj_a, proj_b, h_a, h_b, state_ref, carry_ref, mixed_ref,
                  win_ref, wout_ref, stage_ref, stage_sems, *, tm, tiles_per_seq, n_tiles):
    f32, bf16 = jnp.float32, jnp.bfloat16
    n = pl.program_id(0)
    first, last, even = n == 0, n == n_tiles, n % 2 == 0
    seq_start = (jnp.maximum(n - 1, 0) % tiles_per_seq) == 0

    def normalise(x_ref, h_ref):
        x = x_ref[...]
        ms = jnp.mean(x * x, axis=-1, keepdims=True)
        h_ref[...] = (x * lax.rsqrt(ms + EPS) * gpre_ref[...]).astype(bf16)

    @pl.when(seq_start)
    def _():
        state_ref[...] = jnp.zeros_like(state_ref)
        carry_ref[...] = jnp.zeros_like(carry_ref)

    row = lax.broadcasted_iota(jnp.int32, (CHUNK, CHUNK), 0)
    lane = lax.broadcasted_iota(jnp.int32, (CHUNK, CHUNK), 1)
    tri_t = row <= lane

    def project(h_ref, proj_w, row_parts=1):
        rp = tm // row_parts
        for j in range(0, IN_COLS, PROJ_COLS):
            for r in range(0, tm, rp):
                proj_w[r:r + rp, j:j + PROJ_COLS] = _dot(h_ref[r:r + rp, :], win_ref[:, j:j + PROJ_COLS])
            yield

    def mix(proj_r):
        prev = tuple(carry_ref[i] for i in range(N_CARRY))
        pairs, groups = range(RET_HEADS // 2), range(ATT_KV_HEADS)
        for c in range(tm // CHUNK):
            rows = slice(c * CHUNK, (c + 1) * CHUNK)

            def cols(base, i, width=CHUNK):
                return proj_r[rows, base + i * width:base + (i + 1) * width]

            scored = [_retention_pair_scores(cols(COL_RQ, hp, 2 * RET_DK), cols(COL_RK, hp, 2 * RET_DK))
                      for hp in pairs]
            cur = _attention_prep(cols(COL_AK, 0), cols(COL_AV, 0))
            s_t = [_attention_scores(g, cols(COL_AQ, g, 2 * PAIR), cur, prev) for g in groups]
            yield "scores"
            ret = []
            for hd in range(RET_HEADS):
                sc = scored[hd // 2][0][:, (hd % 2) * CHUNK:(hd % 2 + 1) * CHUNK]
                ret.append(_retention_mix(sc, cols(COL_RQ, hd), cols(COL_RV, hd), state_ref[hd],
                                          decay_ref[hd], xi_ref[hd]))
            for hp in pairs:
                kv = _retention_pair_kv(scored[hp][1], cols(COL_RV, hp, 2 * RET_DV),
                                        zeta_t_ref[2 * hp], zeta_t_ref[2 * hp + 1])
                for e in range(2):
                    hd = 2 * hp + e
                    state_ref[hd] = cdecay_ref[hd] * state_ref[hd] + kv[:, e * RET_DV:(e + 1) * RET_DV]
            yield "mix"
            for hd in range(RET_HEADS):
                mixed_ref[rows, hd * RET_DV:(hd + 1) * RET_DV] = _retention_norm_gate(
                    ret[hd], cols(COL_RG, hd)).astype(bf16)
            first_idx = seq_start.astype(jnp.int32) if c == 0 else 0
            probs = [_attention_probs(g, s_t[g], bias_ref, first_idx, sinks_ref, tri_t)
                     for g in groups]
            yield "softmax"
            pv = [_attention_pv(g, probs[g][0], cur, prev) for g in groups]
            yield "pv"
            for g in groups:
                for p in range(2):
                    i = g * 2 + p
                    heads = (2 * p, 2 * p + 1)
                    pair_t = jnp.concatenate(
                        [pv[g][:, hh * CHUNK:(hh + 1) * CHUNK] * probs[g][1][hh] for hh in heads], axis=0)
                    att = pair_t.T
                    mixed_ref[rows, RET_WIDTH + i * PAIR:RET_WIDTH + (i + 1) * PAIR] = (
                        att * _silu(cols(COL_AG, i))).astype(bf16)
            prev = cur
            yield "gate"
        for i in range(N_CARRY):
            carry_ref[i] = prev[i]

        rp = tm // OUT_ROW_PARTS
        for r in range(0, tm, rp):
            out = _dot(mixed_ref[r:r + rp, :], wout_ref[...])
            ms2 = jnp.mean(out * out, axis=-1, keepdims=True)
            o_ref[r:r + rp, :] = xp_ref[r:r + rp, :] + out * lax.rsqrt(ms2 + EPS) * gpost_ref[...]
            yield "out"

    def step(h_cur, h_next, proj_w, proj_r):
        pieces = project(h_cur, proj_w)
        for _ in range(PIECES_AFTER["start"]):
            next(pieces, None)
        for i, phase in enumerate(mix(proj_r)):
            for _ in range(PIECES_AFTER.get(phase, 0)):
                next(pieces, None)
            if i == NORMALISE_AFTER_PHASE:
                normalise(xnext_ref, h_next)
        for _ in pieces:
            pass

    w_blocks = ([(win_hbm, win_ref, j) for j in range(0, IN_COLS, PROJ_COLS)]
                + [(wout_hbm, wout_ref, j) for j in range(0, D_MODEL, PROJ_COLS)])

    def weight_copy(k):
        src, _, j = w_blocks[k]
        slot = k % N_STAGE
        return pltpu.make_async_copy(src.at[:, pl.ds(j, PROJ_COLS)], stage_ref.at[slot],
                                     stage_sems.at[slot])

    def first_step():
        nb = len(w_blocks)
        for k in range(N_STAGE):
            weight_copy(k).start()
        normalise(xp_ref, h_a)
        pieces = project(h_a, proj_a)
        for g0 in range(0, nb, W_GROUP):
            group = range(g0, min(g0 + W_GROUP, nb))
            for k in group:
                weight_copy(k).wait()
            for k in group:
                _, dst, j = w_blocks[k]
                dst[:, j:j + PROJ_COLS] = stage_ref[k % N_STAGE].astype(bf16)
            for k in group:
                if k + N_STAGE < nb:
                    weight_copy(k + N_STAGE).start()
            if g0 > 0:
                for _ in range(W_GROUP):
                    next(pieces, None)
        for _ in pieces:
            pass
        normalise(xnext_ref, h_b)

    pl.when(first)(first_step)

    @pl.when(last)
    def _():
        for _ in mix(proj_b if n_tiles % 2 == 0 else proj_a):
            pass

    @pl.when(even & jnp.logical_not(first | last))
    def _():
        step(h_a, h_b, proj_a, proj_b)

    @pl.when(jnp.logical_not(even | last))
    def _():
        step(h_b, h_a, proj_b, proj_a)


def _retention_tables():
    f32 = np.float32
    c = CHUNK
    log_gamma = np.log1p(-np.exp2(-5.0 - np.arange(RET_HEADS, dtype=f32))).astype(f32)
    pos = np.arange(c, dtype=f32)
    diff = pos[:, None] - pos[None, :]
    intra = np.where(diff >= 0, np.exp(log_gamma[:, None, None] * np.maximum(diff, f32(0))), f32(0))
    decay = (intra * f32(RET_DK ** -0.5)).astype(f32)
    xi = np.exp(log_gamma[:, None] * (pos + f32(1)))
    zeta = np.exp(log_gamma[:, None] * (f32(c - 1) - pos)) * f32(RET_DK ** -0.5)
    xi_b = np.broadcast_to(xi[:, :, None], (RET_HEADS, c, RET_DK)).astype(f32)
    zeta_t = np.broadcast_to(zeta[:, None, :], (RET_HEADS, RET_DK, c)).astype(f32)
    chunk_decay = np.exp(log_gamma * f32(c)).astype(f32)
    return decay, xi_b, zeta_t, chunk_decay


def _alibi_bias():
    f32 = np.float32
    t = CHUNK
    i = np.arange(t)[:, None]
    j = np.arange(t)[None, :]
    dist = np.where(j <= i, i - j, i + t - j).astype(f32)
    slopes = np.exp2(-8.0 * (np.arange(ATT_HEADS, dtype=f32) + f32(1)) / f32(ATT_HEADS)).astype(f32)
    bias = (f32(LOG2E) * slopes[:, None, None] * dist[None]).astype(f32)
    first = np.where((j <= i)[None], bias, f32(BIG)).astype(f32)
    return np.ascontiguousarray(np.swapaxes(np.stack([bias, first]), -1, -2))


def kernel(x, g_pre, w_in, sinks, w_out, g_post):
    bsz, seq, d_model = x.shape
    assert d_model == D_MODEL and w_in.shape == (D_MODEL, IN_COLS)
    assert w_out.shape == (D_MIX, D_MODEL) and D_MIX == D_MODEL and seq % TM == 0
    f32, bf16 = jnp.float32, jnp.bfloat16
    decay, xi_b, zeta_t, chunk_decay = _retention_tables()
    bias = _alibi_bias()
    n_tiles = bsz * seq // TM
    x2 = x.reshape(bsz * seq, D_MODEL)

    def const(shape):
        return pl.BlockSpec(shape, lambda n: (0,) * len(shape))

    smem = pl.BlockSpec(memory_space=pltpu.SMEM)
    out = pl.pallas_call(
        functools.partial(_layer_kernel, tm=TM, tiles_per_seq=seq // TM, n_tiles=n_tiles),
        grid=(n_tiles + 1,),
        in_specs=[
            smem,
            smem,
            pl.BlockSpec((TM, D_MODEL), lambda n: (jnp.minimum(n + 1, n_tiles - 1), 0)),
            pl.BlockSpec((TM, D_MODEL), lambda n: (jnp.maximum(n - 1, 0), 0)),
            const((1, D_MODEL)),
            pl.BlockSpec(memory_space=pl.ANY),
            pl.BlockSpec(memory_space=pl.ANY),
            const((1, D_MODEL)),
            const((RET_HEADS, CHUNK, CHUNK)),
            const((RET_HEADS, CHUNK, RET_DK)),
            const((RET_HEADS, RET_DK, CHUNK)),
            const((2, ATT_HEADS, CHUNK, CHUNK)),
        ],
        out_specs=pl.BlockSpec((TM, D_MODEL), lambda n: (jnp.maximum(n - 1, 0), 0)),
        out_shape=jax.ShapeDtypeStruct(x2.shape, x.dtype),
        scratch_shapes=[
            pltpu.VMEM((TM, IN_COLS), f32),
            pltpu.VMEM((TM, IN_COLS), f32),
            pltpu.VMEM((TM, D_MODEL), bf16),
            pltpu.VMEM((TM, D_MODEL), bf16),
            pltpu.VMEM((RET_HEADS, RET_DK, RET_DV), f32),
            pltpu.VMEM((N_CARRY, CHUNK, CHUNK), bf16),
            pltpu.VMEM((TM, D_MIX), bf16),
            pltpu.VMEM((D_MODEL, IN_COLS), bf16),
            pltpu.VMEM((D_MIX, D_MODEL), bf16),
            pltpu.VMEM((N_STAGE, D_MODEL, PROJ_COLS), f32),
            pltpu.SemaphoreType.DMA((N_STAGE,)),
        ],
        compiler_params=pltpu.CompilerParams(
            dimension_semantics=("arbitrary",),
            vmem_limit_bytes=VMEM_LIMIT_BYTES),
        name="hybrid_layer",
    )(sinks.astype(f32), chunk_decay, x2, x2, g_pre.reshape(1, D_MODEL).astype(f32),
      w_in.astype(f32), w_out.astype(f32), g_post.reshape(1, D_MODEL).astype(f32),
      decay, xi_b, zeta_t, bias)
    return out.reshape(x.shape)
```

```python
import functools

import jax
import jax.numpy as jnp
import numpy as np
from jax import lax
from jax.experimental import pallas as pl
from jax.experimental.pallas import tpu as pltpu

D_MODEL = 1024
RET_HEADS = 4
RET_DK = 128
RET_DV = 128
RET_WIDTH = RET_HEADS * RET_DV
CHUNK = 128
ATT_HEADS = 8
ATT_KV_HEADS = 2
ATT_GROUP = ATT_HEADS // ATT_KV_HEADS
ATT_HEAD_DIM = 64
ATT_WIDTH = ATT_HEADS * ATT_HEAD_DIM
ATT_KV_WIDTH = ATT_KV_HEADS * ATT_HEAD_DIM
D_MIX = RET_WIDTH + ATT_WIDTH
EPS = 1e-6
BIG = 1e30
LOG2E = 1.4426950408889634

COL_RQ = 0
COL_RK = COL_RQ + RET_HEADS * RET_DK
COL_RV = COL_RK + RET_HEADS * RET_DK
COL_RG = COL_RV + RET_WIDTH
COL_AQ = COL_RG + RET_WIDTH
COL_AK = COL_AQ + ATT_WIDTH
COL_AV = COL_AK + ATT_KV_WIDTH
COL_AG = COL_AV + ATT_KV_WIDTH
IN_COLS = COL_AG + ATT_WIDTH

PAIR = 2 * ATT_HEAD_DIM
assert PAIR == CHUNK and ATT_KV_WIDTH == CHUNK and RET_DK == CHUNK and RET_DV == CHUNK

TM = 512
VMEM_LIMIT_BYTES = 56 * 1024 * 1024
N_CARRY = 5
PROJ_COLS = 256
N_STAGE = 8
OUT_ROW_PARTS = 2
PIECES_AFTER = {"start": 1, "softmax": 1, "gate": 1, "out": 2}
NORMALISE_AFTER_PHASE = 4
assert IN_COLS % PROJ_COLS == 0


def _silu(x):
    hx = 0.5 * x
    return hx + hx * jnp.tanh(hx)


_dot = functools.partial(jnp.dot, preferred_element_type=jnp.float32)


def _retention_pair_scores(q_pair, k_pair):
    bf16 = jnp.bfloat16
    k_ts = (k_pair[:, :RET_DK].T, k_pair[:, RET_DK:].T)
    zeros = jnp.zeros((RET_DK, CHUNK), bf16)
    rhs = jnp.concatenate([jnp.concatenate([k_ts[0].astype(bf16), zeros], axis=1),
                           jnp.concatenate([zeros, k_ts[1].astype(bf16)], axis=1)], axis=0)
    return _dot(q_pair.astype(bf16), rhs), k_ts


def _retention_mix(sc, q, v, st, decay, xi):
    bf16 = jnp.bfloat16
    lhs = jnp.concatenate([(sc * decay).astype(bf16), (q * xi).astype(bf16)], axis=1)
    rhs = jnp.concatenate([v.astype(bf16), st.astype(bf16)], axis=0)
    return _dot(lhs, rhs)


def _retention_pair_kv(k_ts, v_pair, zeta_a, zeta_b):
    bf16 = jnp.bfloat16
    lhs = jnp.concatenate([(k_ts[0] * zeta_a).astype(bf16), (k_ts[1] * zeta_b).astype(bf16)], axis=1)
    vb = v_pair.astype(bf16)
    zeros = jnp.zeros((CHUNK, RET_DV), bf16)
    rhs = jnp.concatenate([jnp.concatenate([vb[:, :RET_DV], zeros], axis=1),
                           jnp.concatenate([zeros, vb[:, RET_DV:]], axis=1)], axis=0)
    return _dot(lhs, rhs)


def _retention_norm_gate(o, gate):
    mu = jnp.mean(o, axis=-1, keepdims=True)
    d = o - mu
    var = jnp.mean(d * d, axis=-1, keepdims=True)
    return d * lax.rsqrt(var + EPS) * _silu(gate)


def _attention_prep(k_both, v_both, lo):
    bf16 = jnp.bfloat16
    k_t = (k_both.T * (ATT_HEAD_DIM ** -0.5 * LOG2E)).astype(bf16)
    v_roll = pltpu.roll(v_both, ATT_HEAD_DIM, axis=1)
    return (k_t,
            jnp.where(lo, v_both, 0.0).astype(bf16),
            jnp.where(lo, 0.0, v_roll).astype(bf16),
            jnp.where(lo, v_roll, 0.0).astype(bf16),
            jnp.where(lo, 0.0, v_both).astype(bf16))


def _attention_scores(g, q_g, cur, prev):
    hs = slice(g * ATT_HEAD_DIM, (g + 1) * ATT_HEAD_DIM)
    k_win = jnp.concatenate([cur[0][hs], prev[0][hs]], axis=1)
    zeros = jnp.zeros_like(k_win)
    rhs_even = jnp.concatenate([k_win, zeros], axis=0)
    rhs_odd = jnp.concatenate([zeros, k_win], axis=0)
    lhs = jnp.concatenate([q_g[:, :PAIR], q_g[:, PAIR:]], axis=0)
    return _dot(lhs, rhs_even), _dot(lhs, rhs_odd)


def _attention_probs(g, s_par, bias_ref, first_idx, sinks_ref, tri, lo):
    bf16 = jnp.bfloat16
    p_rows, sink_terms = [], []
    for p in range(2):
        parts, sink_e = [], []
        for e in range(2):
            hq = g * ATT_GROUP + p * 2 + e
            s = s_par[e][p * CHUNK:(p + 1) * CHUNK]
            sf = jnp.where(tri, s[:, :CHUNK], s[:, CHUNK:]) - bias_ref[first_idx, hq]
            m = jnp.max(sf, axis=-1, keepdims=True)
            ex = jnp.exp2(sf - m)
            sink_e.append(jnp.exp2(sinks_ref[hq] * LOG2E - m))
            parts += [jnp.where(tri, ex, 0.0).astype(bf16), jnp.where(tri, 0.0, ex).astype(bf16)]
        p_rows.append(jnp.concatenate(parts, axis=1))
        sink_terms.append(jnp.where(lo, sink_e[0], sink_e[1]))
    return jnp.concatenate(p_rows, axis=0), sink_terms


def _attention_pv(g, probs, cur, prev, lo):
    bf16 = jnp.bfloat16
    ones_l = jnp.where(lo, 1.0, 0.0).astype(bf16)
    ones_r = jnp.where(lo, 0.0, 1.0).astype(bf16)
    v_l, v_r = cur[1 + 2 * g], cur[2 + 2 * g]
    pv_l, pv_r = prev[1 + 2 * g], prev[2 + 2 * g]
    rhs = jnp.concatenate([jnp.concatenate([v_l, ones_l], axis=1),
                           jnp.concatenate([pv_l, ones_l], axis=1),
                           jnp.concatenate([v_r, ones_r], axis=1),
                           jnp.concatenate([pv_r, ones_r], axis=1)], axis=0)
    return _dot(probs, rhs)


def _layer_kernel(sinks_ref, cdecay_ref, xnext_ref, xp_ref, gpre_ref, win_hbm, wout_hbm,
                  gpost_ref, decay_ref, xi_ref, zeta_t_ref, bias_ref,
                  o_ref, proj_a, proj_b, h_a, h_b, state_ref, carry_ref, mixed_ref,
                  win_ref, wout_ref, stage_ref, stage_sems, *, tm, tiles_per_seq, n_tiles):
    f32, bf16 = jnp.float32, jnp.bfloat16
    n = pl.program_id(0)
    first, last, even = n == 0, n == n_tiles, n % 2 == 0
    seq_start = (jnp.maximum(n - 1, 0) % tiles_per_seq) == 0

    def normalise(x_ref, h_ref):
        x = x_ref[...]
        ms = jnp.mean(x * x, axis=-1, keepdims=True)
        h_ref[...] = (x * lax.rsqrt(ms + EPS) * gpre_ref[...]).astype(bf16)

    @pl.when(seq_start)
    def _():
        state_ref[...] = jnp.zeros_like(state_ref)
        carry_ref[...] = jnp.zeros_like(carry_ref)

    row = lax.broadcasted_iota(jnp.int32, (CHUNK, CHUNK), 0)
    lane = lax.broadcasted_iota(jnp.int32, (CHUNK, CHUNK), 1)
    tri = lane <= row
    lo = lane < ATT_HEAD_DIM

    def project(h_ref, proj_w, row_parts=1):
        rp = tm // row_parts
        for j in range(0, IN_COLS, PROJ_COLS):
            for r in range(0, tm, rp):
                proj_w[r:r + rp, j:j + PROJ_COLS] = _dot(h_ref[r:r + rp, :], win_ref[:, j:j + PROJ_COLS])
            yield

    def mix(proj_r):
        prev = tuple(carry_ref[i] for i in range(N_CARRY))
        pairs, groups = range(RET_HEADS // 2), range(ATT_KV_HEADS)
        for c in range(tm // CHUNK):
            rows = slice(c * CHUNK, (c + 1) * CHUNK)

            def cols(base, i, width=CHUNK):
                return proj_r[rows, base + i * width:base + (i + 1) * width]

            scored = [_retention_pair_scores(cols(COL_RQ, hp, 2 * RET_DK), cols(COL_RK, hp, 2 * RET_DK))
                      for hp in pairs]
            cur = _attention_prep(cols(COL_AK, 0), cols(COL_AV, 0), lo)
            s_par = [_attention_scores(g, cols(COL_AQ, g, 2 * PAIR).astype(bf16), cur, prev)
                     for g in groups]
            yield "scores"
            ret = []
            for hd in range(RET_HEADS):
                sc = scored[hd // 2][0][:, (hd % 2) * CHUNK:(hd % 2 + 1) * CHUNK]
                ret.append(_retention_mix(sc, cols(COL_RQ, hd), cols(COL_RV, hd), state_ref[hd],
                                          decay_ref[hd], xi_ref[hd]))
            for hp in pairs:
                kv = _retention_pair_kv(scored[hp][1], cols(COL_RV, hp, 2 * RET_DV),
                                        zeta_t_ref[2 * hp], zeta_t_ref[2 * hp + 1])
                for e in range(2):
                    hd = 2 * hp + e
                    state_ref[hd] = cdecay_ref[hd] * state_ref[hd] + kv[:, e * RET_DV:(e + 1) * RET_DV]
            yield "mix"
            for hd in range(RET_HEADS):
                mixed_ref[rows, hd * RET_DV:(hd + 1) * RET_DV] = _retention_norm_gate(
                    ret[hd], cols(COL_RG, hd)).astype(bf16)
            first_idx = seq_start.astype(jnp.int32) if c == 0 else 0
            probs = [_attention_probs(g, s_par[g], bias_ref, first_idx, sinks_ref, tri, lo)
                     for g in groups]
            yield "softmax"
            pv = [_attention_pv(g, probs[g][0], cur, prev, lo) for g in groups]
            yield "pv"
            for g in groups:
                for p in range(2):
                    i = g * 2 + p
                    o = pv[g][p * CHUNK:(p + 1) * CHUNK]
                    att = o[:, :PAIR] / (o[:, PAIR:] + probs[g][1][p])
                    mixed_ref[rows, RET_WIDTH + i * PAIR:RET_WIDTH + (i + 1) * PAIR] = (
                        att * _silu(cols(COL_AG, i))).astype(bf16)
            prev = cur
            yield "gate"
        for i in range(N_CARRY):
            carry_ref[i] = prev[i]

        rp = tm // OUT_ROW_PARTS
        for r in range(0, tm, rp):
            out = _dot(mixed_ref[r:r + rp, :], wout_ref[...])
            ms2 = jnp.mean(out * out, axis=-1, keepdims=True)
            o_ref[r:r + rp, :] = xp_ref[r:r + rp, :] + out * lax.rsqrt(ms2 + EPS) * gpost_ref[...]
            yield "out"

    def step(h_cur, h_next, proj_w, proj_r):
        pieces = project(h_cur, proj_w, row_parts=2)
        for _ in range(PIECES_AFTER["start"]):
            next(pieces, None)
        for i, phase in enumerate(mix(proj_r)):
            for _ in range(PIECES_AFTER.get(phase, 0)):
                next(pieces, None)
            if i == NORMALISE_AFTER_PHASE:
                normalise(xnext_ref, h_next)
        for _ in pieces:
            pass

    w_blocks = ([(win_hbm, win_ref, j) for j in range(0, IN_COLS, PROJ_COLS)]
                + [(wout_hbm, wout_ref, j) for j in range(0, D_MODEL, PROJ_COLS)])

    def weight_copy(k):
        src, _, j = w_blocks[k]
        slot = k % N_STAGE
        return pltpu.make_async_copy(src.at[:, pl.ds(j, PROJ_COLS)], stage_ref.at[slot],
                                     stage_sems.at[slot])

    def first_step():
        for k in range(N_STAGE):
            weight_copy(k).start()
        normalise(xp_ref, h_a)
        pieces = project(h_a, proj_a, row_parts=2)
        for k, (_, dst, j) in enumerate(w_blocks):
            weight_copy(k).wait()
            dst[:, j:j + PROJ_COLS] = stage_ref[k % N_STAGE].astype(bf16)
            if k + N_STAGE < len(w_blocks):
                weight_copy(k + N_STAGE).start()
            if k >= 1:
                next(pieces, None)
        for _ in pieces:
            pass
        normalise(xnext_ref, h_b)

    pl.when(first)(first_step)

    @pl.when(last)
    def _():
        for _ in mix(proj_b if n_tiles % 2 == 0 else proj_a):
            pass

    @pl.when(even & jnp.logical_not(first | last))
    def _():
        step(h_a, h_b, proj_a, proj_b)

    @pl.when(jnp.logical_not(even | last))
    def _():
        step(h_b, h_a, proj_b, proj_a)


def _retention_tables():
    f32 = np.float32
    c = CHUNK
    log_gamma = np.log1p(-np.exp2(-5.0 - np.arange(RET_HEADS, dtype=f32))).astype(f32)
    pos = np.arange(c, dtype=f32)
    diff = pos[:, None] - pos[None, :]
    intra = np.where(diff >= 0, np.exp(log_gamma[:, None, None] * np.maximum(diff, f32(0))), f32(0))
    decay = (intra * f32(RET_DK ** -0.5)).astype(f32)
    xi = np.exp(log_gamma[:, None] * (pos + f32(1)))
    zeta = np.exp(log_gamma[:, None] * (f32(c - 1) - pos)) * f32(RET_DK ** -0.5)
    xi_b = np.broadcast_to(xi[:, :, None], (RET_HEADS, c, RET_DK)).astype(f32)
    zeta_t = np.broadcast_to(zeta[:, None, :], (RET_HEADS, RET_DK, c)).astype(f32)
    chunk_decay = np.exp(log_gamma * f32(c)).astype(f32)
    return decay, xi_b, zeta_t, chunk_decay


def _alibi_bias():
    f32 = np.float32
    t = CHUNK
    i = np.arange(t)[:, None]
    j = np.arange(t)[None, :]
    dist = np.where(j <= i, i - j, i + t - j).astype(f32)
    slopes = np.exp2(-8.0 * (np.arange(ATT_HEADS, dtype=f32) + f32(1)) / f32(ATT_HEADS)).astype(f32)
    bias = (f32(LOG2E) * slopes[:, None, None] * dist[None]).astype(f32)
    first = np.where((j <= i)[None], bias, f32(BIG)).astype(f32)
    return np.stack([bias, first])


def kernel(x, g_pre, w_in, sinks, w_out, g_post):
    bsz, seq, d_model = x.shape
    assert d_model == D_MODEL and w_in.shape == (D_MODEL, IN_COLS)
    assert w_out.shape == (D_MIX, D_MODEL) and D_MIX == D_MODEL and seq % TM == 0
    f32, bf16 = jnp.float32, jnp.bfloat16
    decay, xi_b, zeta_t, chunk_decay = _retention_tables()
    bias = _alibi_bias()
    n_tiles = bsz * seq // TM
    x2 = x.reshape(bsz * seq, D_MODEL)

    def const(shape):
        return pl.BlockSpec(shape, lambda n: (0,) * len(shape))

    smem = pl.BlockSpec(memory_space=pltpu.SMEM)
    out = pl.pallas_call(
        functools.partial(_layer_kernel, tm=TM, tiles_per_seq=seq // TM, n_tiles=n_tiles),
        grid=(n_tiles + 1,),
        in_specs=[
            smem,
            smem,
            pl.BlockSpec((TM, D_MODEL), lambda n: (jnp.minimum(n + 1, n_tiles - 1), 0)),
            pl.BlockSpec((TM, D_MODEL), lambda n: (jnp.maximum(n - 1, 0), 0)),
            const((1, D_MODEL)),
            pl.BlockSpec(memory_space=pl.ANY),
            pl.BlockSpec(memory_space=pl.ANY),
            const((1, D_MODEL)),
            const((RET_HEADS, CHUNK, CHUNK)),
            const((RET_HEADS, CHUNK, RET_DK)),
            const((RET_HEADS, RET_DK, CHUNK)),
            const((2, ATT_HEADS, CHUNK, CHUNK)),
        ],
        out_specs=pl.BlockSpec((TM, D_MODEL), lambda n: (jnp.maximum(n - 1, 0), 0)),
        out_shape=jax.ShapeDtypeStruct(x2.shape, x.dtype),
        scratch_shapes=[
            pltpu.VMEM((TM, IN_COLS), f32),
            pltpu.VMEM((TM, IN_COLS), f32),
            pltpu.VMEM((TM, D_MODEL), bf16),
            pltpu.VMEM((TM, D_MODEL), bf16),
            pltpu.VMEM((RET_HEADS, RET_DK, RET_DV), f32),
            pltpu.VMEM((N_CARRY, CHUNK, CHUNK), bf16),
            pltpu.VMEM((TM, D_MIX), bf16),
            pltpu.VMEM((D_MODEL, IN_COLS), bf16),
            pltpu.VMEM((D_MIX, D_MODEL), bf16),
            pltpu.VMEM((N_STAGE, D_MODEL, PROJ_COLS), f32),
            pltpu.SemaphoreType.DMA((N_STAGE,)),
        ],
        compiler_params=pltpu.CompilerParams(
            dimension_semantics=("arbitrary",),
            vmem_limit_bytes=VMEM_LIMIT_BYTES),
        name="hybrid_layer",
    )(sinks.astype(f32), chunk_decay, x2, x2, g_pre.reshape(1, D_MODEL).astype(f32),
      w_in.astype(f32), w_out.astype(f32), g_post.reshape(1, D_MODEL).astype(f32),
      decay, xi_b, zeta_t, bias)
    return out.reshape(x.shape)
```

```python
import functools

import jax
import jax.numpy as jnp
import numpy as np
from jax import lax
from jax.experimental import pallas as pl
from jax.experimental.pallas import tpu as pltpu

D_MODEL = 1024
RET_HEADS = 4
RET_DK = 128
RET_DV = 128
RET_WIDTH = RET_HEADS * RET_DV
CHUNK = 128
ATT_HEADS = 8
ATT_KV_HEADS = 2
ATT_GROUP = ATT_HEADS // ATT_KV_HEADS
ATT_HEAD_DIM = 64
ATT_WIDTH = ATT_HEADS * ATT_HEAD_DIM
ATT_KV_WIDTH = ATT_KV_HEADS * ATT_HEAD_DIM
D_MIX = RET_WIDTH + ATT_WIDTH
EPS = 1e-6
BIG = 1e30
LOG2E = 1.4426950408889634

COL_RQ = 0
COL_RK = COL_RQ + RET_HEADS * RET_DK
COL_RV = COL_RK + RET_HEADS * RET_DK
COL_RG = COL_RV + RET_WIDTH
COL_AQ = COL_RG + RET_WIDTH
COL_AK = COL_AQ + ATT_WIDTH
COL_AV = COL_AK + ATT_KV_WIDTH
COL_AG = COL_AV + ATT_KV_WIDTH
IN_COLS = COL_AG + ATT_WIDTH

PAIR = 2 * ATT_HEAD_DIM
assert PAIR == CHUNK and ATT_KV_WIDTH == CHUNK and RET_DK == CHUNK and RET_DV == CHUNK

TM = 512
VMEM_LIMIT_BYTES = 56 * 1024 * 1024
N_CARRY = 2
PROJ_COLS = 256
N_STAGE = 8
OUT_ROW_PARTS = 2
PIECES_AFTER = {"start": 1, "softmax": 1, "gate": 1, "out": 0}
NORMALISE_AFTER_PHASE = 4
assert IN_COLS % PROJ_COLS == 0


def _silu(x):
    hx = 0.5 * x
    return hx + hx * jnp.tanh(hx)


_dot = functools.partial(jnp.dot, preferred_element_type=jnp.float32)


def _retention_pair_scores(q_pair, k_pair):
    bf16 = jnp.bfloat16
    k_ts = (k_pair[:, :RET_DK].T, k_pair[:, RET_DK:].T)
    zeros = jnp.zeros((RET_DK, CHUNK), bf16)
    rhs = jnp.concatenate([jnp.concatenate([k_ts[0].astype(bf16), zeros], axis=1),
                           jnp.concatenate([zeros, k_ts[1].astype(bf16)], axis=1)], axis=0)
    return _dot(q_pair.astype(bf16), rhs), k_ts


def _retention_mix(sc, q, v, st, decay, xi):
    bf16 = jnp.bfloat16
    lhs = jnp.concatenate([(sc * decay).astype(bf16), (q * xi).astype(bf16)], axis=1)
    rhs = jnp.concatenate([v.astype(bf16), st.astype(bf16)], axis=0)
    return _dot(lhs, rhs)


def _retention_pair_kv(k_ts, v_pair, zeta_a, zeta_b):
    bf16 = jnp.bfloat16
    lhs = jnp.concatenate([(k_ts[0] * zeta_a).astype(bf16), (k_ts[1] * zeta_b).astype(bf16)], axis=1)
    vb = v_pair.astype(bf16)
    zeros = jnp.zeros((CHUNK, RET_DV), bf16)
    rhs = jnp.concatenate([jnp.concatenate([vb[:, :RET_DV], zeros], axis=1),
                           jnp.concatenate([zeros, vb[:, RET_DV:]], axis=1)], axis=0)
    return _dot(lhs, rhs)


def _retention_norm_gate(o, gate):
    mu = jnp.mean(o, axis=-1, keepdims=True)
    d = o - mu
    var = jnp.mean(d * d, axis=-1, keepdims=True)
    return d * lax.rsqrt(var + EPS) * _silu(gate)


def _attention_prep(k_both, v_both):
    bf16 = jnp.bfloat16
    return (k_both * (ATT_HEAD_DIM ** -0.5 * LOG2E)).astype(bf16), v_both.T.astype(bf16)


def _attention_scores(g, q_g, cur, prev):
    bf16 = jnp.bfloat16
    k_win = jnp.concatenate([cur[0], prev[0]], axis=0)
    zeros = jnp.zeros((ATT_HEAD_DIM, CHUNK), bf16)
    cols = []
    for p in range(2):
        q_t = q_g[:, p * PAIR:(p + 1) * PAIR].T.astype(bf16)
        for e in range(2):
            q_h = q_t[e * ATT_HEAD_DIM:(e + 1) * ATT_HEAD_DIM]
            cols.append(jnp.concatenate([q_h, zeros] if g == 0 else [zeros, q_h], axis=0))
    return _dot(k_win, jnp.concatenate(cols, axis=1))


def _attention_probs(g, s_t, bias_t_ref, first_idx, sinks_ref, tri_t):
    bf16 = jnp.bfloat16
    parts, inv_dens = [], []
    for hh in range(ATT_GROUP):
        hq = g * ATT_GROUP + hh
        s = s_t[:, hh * CHUNK:(hh + 1) * CHUNK]
        sf = jnp.where(tri_t, s[:CHUNK], s[CHUNK:]) - bias_t_ref[first_idx, hq]
        m = jnp.max(sf, axis=0, keepdims=True)
        ex = jnp.exp2(sf - m)
        den = jnp.sum(ex, axis=0, keepdims=True) + jnp.exp2(sinks_ref[hq] * LOG2E - m)
        inv_dens.append(1.0 / den)
        parts.append(jnp.concatenate([jnp.where(tri_t, ex, 0.0).astype(bf16),
                                      jnp.where(tri_t, 0.0, ex).astype(bf16)], axis=0))
    return jnp.concatenate(parts, axis=1), inv_dens


def _attention_pv(g, probs_t, cur, prev):
    out = _dot(jnp.concatenate([cur[1], prev[1]], axis=1), probs_t)
    return out[g * ATT_HEAD_DIM:(g + 1) * ATT_HEAD_DIM]


def _layer_kernel(sinks_ref, cdecay_ref, xnext_ref, xp_ref, gpre_ref, win_hbm, wout_hbm,
                  gpost_ref, decay_ref, xi_ref, zeta_t_ref, bias_ref,
                  o_ref, proj_a, proj_b, h_a, h_b, state_ref, carry_ref, mixed_a, mixed_b,
                  win_ref, wout_ref, stage_ref, stage_sems, *, tm, tiles_per_seq, n_tiles):
    f32, bf16 = jnp.float32, jnp.bfloat16
    n = pl.program_id(0)
    even = n % 2 == 0
    seq_start = (jnp.maximum(n - 1, 0) % tiles_per_seq) == 0

    def normalise(x_ref, h_ref):
        x = x_ref[...]
        ms = jnp.mean(x * x, axis=-1, keepdims=True)
        h_ref[...] = (x * lax.rsqrt(ms + EPS) * gpre_ref[...]).astype(bf16)

    @pl.when(seq_start)
    def _():
        state_ref[...] = jnp.zeros_like(state_ref)
        carry_ref[...] = jnp.zeros_like(carry_ref)

    row = lax.broadcasted_iota(jnp.int32, (CHUNK, CHUNK), 0)
    lane = lax.broadcasted_iota(jnp.int32, (CHUNK, CHUNK), 1)
    tri_t = row <= lane

    def project(h_ref, proj_w, row_parts=1):
        rp = tm // row_parts
        for j in range(0, IN_COLS, PROJ_COLS):
            for r in range(0, tm, rp):
                proj_w[r:r + rp, j:j + PROJ_COLS] = _dot(h_ref[r:r + rp, :], win_ref[:, j:j + PROJ_COLS])
            yield

    def mix(proj_r, mixed_ref):
        prev = tuple(carry_ref[i] for i in range(N_CARRY))
        pairs, groups = range(RET_HEADS // 2), range(ATT_KV_HEADS)
        for c in range(tm // CHUNK):
            rows = slice(c * CHUNK, (c + 1) * CHUNK)

            def cols(base, i, width=CHUNK):
                return proj_r[rows, base + i * width:base + (i + 1) * width]

            scored = [_retention_pair_scores(cols(COL_RQ, hp, 2 * RET_DK), cols(COL_RK, hp, 2 * RET_DK))
                      for hp in pairs]
            cur = _attention_prep(cols(COL_AK, 0), cols(COL_AV, 0))
            s_t = [_attention_scores(g, cols(COL_AQ, g, 2 * PAIR), cur, prev) for g in groups]
            yield "scores"
            ret = []
            for hd in range(RET_HEADS):
                sc = scored[hd // 2][0][:, (hd % 2) * CHUNK:(hd % 2 + 1) * CHUNK]
                ret.append(_retention_mix(sc, cols(COL_RQ, hd), cols(COL_RV, hd), state_ref[hd],
                                          decay_ref[hd], xi_ref[hd]))
            for hp in pairs:
                kv = _retention_pair_kv(scored[hp][1], cols(COL_RV, hp, 2 * RET_DV),
                                        zeta_t_ref[2 * hp], zeta_t_ref[2 * hp + 1])
                for e in range(2):
                    hd = 2 * hp + e
                    state_ref[hd] = cdecay_ref[hd] * state_ref[hd] + kv[:, e * RET_DV:(e + 1) * RET_DV]
            yield "mix"
            for hd in range(RET_HEADS):
                mixed_ref[rows, hd * RET_DV:(hd + 1) * RET_DV] = _retention_norm_gate(
                    ret[hd], cols(COL_RG, hd)).astype(bf16)
            first_idx = seq_start.astype(jnp.int32) if c == 0 else 0
            probs = [_attention_probs(g, s_t[g], bias_ref, first_idx, sinks_ref, tri_t)
                     for g in groups]
            yield "softmax"
            pv = [_attention_pv(g, probs[g][0], cur, prev) for g in groups]
            yield "pv"
            for g in groups:
                for p in range(2):
                    i = g * 2 + p
                    heads = (2 * p, 2 * p + 1)
                    pair_t = jnp.concatenate(
                        [pv[g][:, hh * CHUNK:(hh + 1) * CHUNK] * probs[g][1][hh] for hh in heads], axis=0)
                    att = pair_t.T
                    mixed_ref[rows, RET_WIDTH + i * PAIR:RET_WIDTH + (i + 1) * PAIR] = (
                        att * _silu(cols(COL_AG, i))).astype(bf16)
            prev = cur
            yield "gate"
        for i in range(N_CARRY):
            carry_ref[i] = prev[i]

    def finish(mixed_ref):
        rp = tm // OUT_ROW_PARTS
        for r in range(0, tm, rp):
            out = _dot(mixed_ref[r:r + rp, :], wout_ref[...])
            ms2 = jnp.mean(out * out, axis=-1, keepdims=True)
            o_ref[r:r + rp, :] = xp_ref[r:r + rp, :] + out * lax.rsqrt(ms2 + EPS) * gpost_ref[...]
            yield "out"

    def step(h_cur, h_next, proj_w, proj_r, mixed_w, mixed_r, do_project=True, do_finish=True):
        pieces = project(h_cur, proj_w) if do_project else iter(())
        if do_finish:
            for phase in finish(mixed_r):
                for _ in range(PIECES_AFTER.get(phase, 0)):
                    next(pieces, None)
        for _ in range(PIECES_AFTER["start"]):
            next(pieces, None)
        for i, phase in enumerate(mix(proj_r, mixed_w)):
            for _ in range(PIECES_AFTER.get(phase, 0)):
                next(pieces, None)
            if i == NORMALISE_AFTER_PHASE and do_project:
                normalise(xnext_ref, h_next)
        for _ in pieces:
            pass

    w_blocks = ([(win_hbm, win_ref, j) for j in range(0, IN_COLS, PROJ_COLS)]
                + [(wout_hbm, wout_ref, j) for j in range(0, D_MODEL, PROJ_COLS)])

    def weight_copy(k):
        src, _, j = w_blocks[k]
        slot = k % N_STAGE
        return pltpu.make_async_copy(src.at[:, pl.ds(j, PROJ_COLS)], stage_ref.at[slot],
                                     stage_sems.at[slot])

    def first_step():
        for k in range(N_STAGE):
            weight_copy(k).start()
        normalise(xp_ref, h_a)
        pieces = project(h_a, proj_a, row_parts=2)
        for k, (_, dst, j) in enumerate(w_blocks):
            weight_copy(k).wait()
            dst[:, j:j + PROJ_COLS] = stage_ref[k % N_STAGE].astype(bf16)
            if k + N_STAGE < len(w_blocks):
                weight_copy(k + N_STAGE).start()
            if k >= 1:
                next(pieces, None)
        for _ in pieces:
            pass
        normalise(xnext_ref, h_b)

    pl.when(n == 0)(first_step)

    @pl.when(n == 1)
    def _():
        step(h_b, h_a, proj_b, proj_a, mixed_a, mixed_b, do_finish=False)

    @pl.when(even & (n >= 2) & (n < n_tiles))
    def _():
        step(h_a, h_b, proj_a, proj_b, mixed_b, mixed_a)

    @pl.when(jnp.logical_not(even) & (n >= 3) & (n < n_tiles))
    def _():
        step(h_b, h_a, proj_b, proj_a, mixed_a, mixed_b)

    @pl.when(n == n_tiles)
    def _():
        step(h_a, h_b, proj_a, proj_b, mixed_b, mixed_a, do_project=False)

    @pl.when(n == n_tiles + 1)
    def _():
        for _ in finish(mixed_b):
            pass


def _retention_tables():
    f32 = np.float32
    c = CHUNK
    log_gamma = np.log1p(-np.exp2(-5.0 - np.arange(RET_HEADS, dtype=f32))).astype(f32)
    pos = np.arange(c, dtype=f32)
    diff = pos[:, None] - pos[None, :]
    intra = np.where(diff >= 0, np.exp(log_gamma[:, None, None] * np.maximum(diff, f32(0))), f32(0))
    decay = (intra * f32(RET_DK ** -0.5)).astype(f32)
    xi = np.exp(log_gamma[:, None] * (pos + f32(1)))
    zeta = np.exp(log_gamma[:, None] * (f32(c - 1) - pos)) * f32(RET_DK ** -0.5)
    xi_b = np.broadcast_to(xi[:, :, None], (RET_HEADS, c, RET_DK)).astype(f32)
    zeta_t = np.broadcast_to(zeta[:, None, :], (RET_HEADS, RET_DK, c)).astype(f32)
    chunk_decay = np.exp(log_gamma * f32(c)).astype(f32)
    return decay, xi_b, zeta_t, chunk_decay


def _alibi_bias():
    f32 = np.float32
    t = CHUNK
    i = np.arange(t)[:, None]
    j = np.arange(t)[None, :]
    dist = np.where(j <= i, i - j, i + t - j).astype(f32)
    slopes = np.exp2(-8.0 * (np.arange(ATT_HEADS, dtype=f32) + f32(1)) / f32(ATT_HEADS)).astype(f32)
    bias = (f32(LOG2E) * slopes[:, None, None] * dist[None]).astype(f32)
    first = np.where((j <= i)[None], bias, f32(BIG)).astype(f32)
    return np.ascontiguousarray(np.swapaxes(np.stack([bias, first]), -1, -2))


def kernel(x, g_pre, w_in, sinks, w_out, g_post):
    bsz, seq, d_model = x.shape
    assert d_model == D_MODEL and w_in.shape == (D_MODEL, IN_COLS)
    assert w_out.shape == (D_MIX, D_MODEL) and D_MIX == D_MODEL and seq % TM == 0
    assert (bsz * seq // TM) % 2 == 0 and bsz * seq // TM >= 4
    f32, bf16 = jnp.float32, jnp.bfloat16
    decay, xi_b, zeta_t, chunk_decay = _retention_tables()
    bias = _alibi_bias()
    n_tiles = bsz * seq // TM
    x2 = x.reshape(bsz * seq, D_MODEL)

    def const(shape):
        return pl.BlockSpec(shape, lambda n: (0,) * len(shape))

    smem = pl.BlockSpec(memory_space=pltpu.SMEM)
    out = pl.pallas_call(
        functools.partial(_layer_kernel, tm=TM, tiles_per_seq=seq // TM, n_tiles=n_tiles),
        grid=(n_tiles + 2,),
        in_specs=[
            smem,
            smem,
            pl.BlockSpec((TM, D_MODEL), lambda n: (jnp.minimum(n + 1, n_tiles - 1), 0)),
            pl.BlockSpec((TM, D_MODEL), lambda n: (jnp.clip(n - 2, 0, n_tiles - 1), 0)),
            const((1, D_MODEL)),
            pl.BlockSpec(memory_space=pl.ANY),
            pl.BlockSpec(memory_space=pl.ANY),
            const((1, D_MODEL)),
            const((RET_HEADS, CHUNK, CHUNK)),
            const((RET_HEADS, CHUNK, RET_DK)),
            const((RET_HEADS, RET_DK, CHUNK)),
            const((2, ATT_HEADS, CHUNK, CHUNK)),
        ],
        out_specs=pl.BlockSpec((TM, D_MODEL), lambda n: (jnp.clip(n - 2, 0, n_tiles - 1), 0)),
        out_shape=jax.ShapeDtypeStruct(x2.shape, x.dtype),
        scratch_shapes=[
            pltpu.VMEM((TM, IN_COLS), f32),
            pltpu.VMEM((TM, IN_COLS), f32),
            pltpu.VMEM((TM, D_MODEL), bf16),
            pltpu.VMEM((TM, D_MODEL), bf16),
            pltpu.VMEM((RET_HEADS, RET_DK, RET_DV), f32),
            pltpu.VMEM((N_CARRY, CHUNK, CHUNK), bf16),
            pltpu.VMEM((TM, D_MIX), bf16),
            pltpu.VMEM((TM, D_MIX), bf16),
            pltpu.VMEM((D_MODEL, IN_COLS), bf16),
            pltpu.VMEM((D_MIX, D_MODEL), bf16),
            pltpu.VMEM((N_STAGE, D_MODEL, PROJ_COLS), f32),
            pltpu.SemaphoreType.DMA((N_STAGE,)),
        ],
        compiler_params=pltpu.CompilerParams(
            dimension_semantics=("arbitrary",),
            vmem_limit_bytes=VMEM_LIMIT_BYTES),
        name="hybrid_layer",
    )(sinks.astype(f32), chunk_decay, x2, x2, g_pre.reshape(1, D_MODEL).astype(f32),
      w_in.astype(f32), w_out.astype(f32), g_post.reshape(1, D_MODEL).astype(f32),
      decay, xi_b, zeta_t, bias)
    return out.reshape(x.shape)
```

```python
import functools

import jax
import jax.numpy as jnp
import numpy as np
from jax import lax
from jax.experimental import pallas as pl
from jax.experimental.pallas import tpu as pltpu

D_MODEL = 1024
RET_HEADS = 4
RET_DK = 128
RET_DV = 128
RET_WIDTH = RET_HEADS * RET_DV
CHUNK = 128
ATT_HEADS = 8
ATT_KV_HEADS = 2
ATT_GROUP = ATT_HEADS // ATT_KV_HEADS
ATT_HEAD_DIM = 64
ATT_WIDTH = ATT_HEADS * ATT_HEAD_DIM
ATT_KV_WIDTH = ATT_KV_HEADS * ATT_HEAD_DIM
D_MIX = RET_WIDTH + ATT_WIDTH
EPS = 1e-6
BIG = 1e30
LOG2E = 1.4426950408889634

COL_RQ = 0
COL_RK = COL_RQ + RET_HEADS * RET_DK
COL_RV = COL_RK + RET_HEADS * RET_DK
COL_RG = COL_RV + RET_WIDTH
COL_AQ = COL_RG + RET_WIDTH
COL_AK = COL_AQ + ATT_WIDTH
COL_AV = COL_AK + ATT_KV_WIDTH
COL_AG = COL_AV + ATT_KV_WIDTH
IN_COLS = COL_AG + ATT_WIDTH

PAIR = 2 * ATT_HEAD_DIM
assert PAIR == CHUNK and ATT_KV_WIDTH == CHUNK and RET_DK == CHUNK and RET_DV == CHUNK

TM = 512
VMEM_LIMIT_BYTES = 56 * 1024 * 1024
N_CARRY = 2
PROJ_COLS = 256
N_STAGE = 8
OUT_ROW_PARTS = 2
PIECES_AFTER = {"start": 1, "softmax": 1, "gate": 1, "out": 2}
NORMALISE_AFTER_PHASE = 4
assert IN_COLS % PROJ_COLS == 0


def _silu(x):
    hx = 0.5 * x
    return hx + hx * jnp.tanh(hx)


_dot = functools.partial(jnp.dot, preferred_element_type=jnp.float32)


def _retention_pair_scores(q_pair, k_pair):
    bf16 = jnp.bfloat16
    k_ts = (k_pair[:, :RET_DK].T, k_pair[:, RET_DK:].T)
    zeros = jnp.zeros((RET_DK, CHUNK), bf16)
    rhs = jnp.concatenate([jnp.concatenate([k_ts[0].astype(bf16), zeros], axis=1),
                           jnp.concatenate([zeros, k_ts[1].astype(bf16)], axis=1)], axis=0)
    return _dot(q_pair.astype(bf16), rhs), k_ts


def _retention_mix(sc, q, v, st, decay, xi):
    bf16 = jnp.bfloat16
    lhs = jnp.concatenate([(sc * decay).astype(bf16), (q * xi).astype(bf16)], axis=1)
    rhs = jnp.concatenate([v.astype(bf16), st.astype(bf16)], axis=0)
    return _dot(lhs, rhs)


def _retention_pair_kv(k_ts, v_pair, zeta_a, zeta_b):
    bf16 = jnp.bfloat16
    lhs = jnp.concatenate([(k_ts[0] * zeta_a).astype(bf16), (k_ts[1] * zeta_b).astype(bf16)], axis=1)
    vb = v_pair.astype(bf16)
    zeros = jnp.zeros((CHUNK, RET_DV), bf16)
    rhs = jnp.concatenate([jnp.concatenate([vb[:, :RET_DV], zeros], axis=1),
                           jnp.concatenate([zeros, vb[:, RET_DV:]], axis=1)], axis=0)
    return _dot(lhs, rhs)


def _retention_norm_gate(o, gate):
    mu = jnp.mean(o, axis=-1, keepdims=True)
    d = o - mu
    var = jnp.mean(d * d, axis=-1, keepdims=True)
    return d * lax.rsqrt(var + EPS) * _silu(gate)


def _attention_prep(k_both, v_both):
    bf16 = jnp.bfloat16
    return (k_both * (ATT_HEAD_DIM ** -0.5 * LOG2E)).astype(bf16), v_both.T.astype(bf16)


def _attention_scores(g, q_g, cur, prev):
    bf16 = jnp.bfloat16
    k_win = jnp.concatenate([cur[0], prev[0]], axis=0)
    zeros = jnp.zeros((ATT_HEAD_DIM, CHUNK), bf16)
    cols = []
    for p in range(2):
        q_t = q_g[:, p * PAIR:(p + 1) * PAIR].T.astype(bf16)
        for e in range(2):
            q_h = q_t[e * ATT_HEAD_DIM:(e + 1) * ATT_HEAD_DIM]
            cols.append(jnp.concatenate([q_h, zeros] if g == 0 else [zeros, q_h], axis=0))
    return _dot(k_win, jnp.concatenate(cols, axis=1))


def _attention_probs(g, s_t, bias_t_ref, first_idx, sinks_ref, tri_t):
    bf16 = jnp.bfloat16
    parts, inv_dens = [], []
    for hh in range(ATT_GROUP):
        hq = g * ATT_GROUP + hh
        s = s_t[:, hh * CHUNK:(hh + 1) * CHUNK]
        sf = jnp.where(tri_t, s[:CHUNK], s[CHUNK:]) - bias_t_ref[first_idx, hq]
        m = jnp.max(sf, axis=0, keepdims=True)
        ex = jnp.exp2(sf - m)
        den = jnp.sum(ex, axis=0, keepdims=True) + jnp.exp2(sinks_ref[hq] * LOG2E - m)
        inv_dens.append(1.0 / den)
        parts.append(jnp.concatenate([jnp.where(tri_t, ex, 0.0).astype(bf16),
                                      jnp.where(tri_t, 0.0, ex).astype(bf16)], axis=0))
    return jnp.concatenate(parts, axis=1), inv_dens


def _attention_pv(g, probs_t, cur, prev):
    out = _dot(jnp.concatenate([cur[1], prev[1]], axis=1), probs_t)
    return out[g * ATT_HEAD_DIM:(g + 1) * ATT_HEAD_DIM]


def _layer_kernel(sinks_ref, cdecay_ref, xnext_ref, xp_ref, gpre_ref, win_hbm, wout_hbm,
                  gpost_ref, decay_ref, xi_ref, zeta_t_ref, bias_ref,
                  o_ref, proj_a, proj_b, h_a, h_b, state_ref, carry_ref, mixed_ref,
                  win_ref, wout_ref, stage_ref, stage_sems, *, tm, tiles_per_seq, n_tiles):
    f32, bf16 = jnp.float32, jnp.bfloat16
    n = pl.program_id(0)
    first, last, even = n == 0, n == n_tiles, n % 2 == 0
    seq_start = (jnp.maximum(n - 1, 0) % tiles_per_seq) == 0

    def normalise(x_ref, h_ref):
        x = x_ref[...]
        ms = jnp.mean(x * x, axis=-1, keepdims=True)
        h_ref[...] = (x * lax.rsqrt(ms + EPS) * gpre_ref[...]).astype(bf16)

    @pl.when(seq_start)
    def _():
        state_ref[...] = jnp.zeros_like(state_ref)
        carry_ref[...] = jnp.zeros_like(carry_ref)

    row = lax.broadcasted_iota(jnp.int32, (CHUNK, CHUNK), 0)
    lane = lax.broadcasted_iota(jnp.int32, (CHUNK, CHUNK), 1)
    tri_t = row <= lane

    def project(h_ref, proj_w):
        for j in range(0, IN_COLS, PROJ_COLS):
            proj_w[:, j:j + PROJ_COLS] = _dot(h_ref[...], win_ref[:, j:j + PROJ_COLS])
            yield

    def mix(proj_r):
        prev = tuple(carry_ref[i] for i in range(N_CARRY))
        pairs, groups = range(RET_HEADS // 2), range(ATT_KV_HEADS)
        for c in range(tm // CHUNK):
            rows = slice(c * CHUNK, (c + 1) * CHUNK)

            def cols(base, i, width=CHUNK):
                return proj_r[rows, base + i * width:base + (i + 1) * width]

            scored = [_retention_pair_scores(cols(COL_RQ, hp, 2 * RET_DK), cols(COL_RK, hp, 2 * RET_DK))
                      for hp in pairs]
            cur = _attention_prep(cols(COL_AK, 0), cols(COL_AV, 0))
            s_t = [_attention_scores(g, cols(COL_AQ, g, 2 * PAIR), cur, prev) for g in groups]
            yield "scores"
            ret = []
            for hd in range(RET_HEADS):
                sc = scored[hd // 2][0][:, (hd % 2) * CHUNK:(hd % 2 + 1) * CHUNK]
                ret.append(_retention_mix(sc, cols(COL_RQ, hd), cols(COL_RV, hd), state_ref[hd],
                                          decay_ref[hd], xi_ref[hd]))
            for hp in pairs:
                kv = _retention_pair_kv(scored[hp][1], cols(COL_RV, hp, 2 * RET_DV),
                                        zeta_t_ref[2 * hp], zeta_t_ref[2 * hp + 1])
                for e in range(2):
                    hd = 2 * hp + e
                    state_ref[hd] = cdecay_ref[hd] * state_ref[hd] + kv[:, e * RET_DV:(e + 1) * RET_DV]
            yield "mix"
            for hd in range(RET_HEADS):
                mixed_ref[rows, hd * RET_DV:(hd + 1) * RET_DV] = _retention_norm_gate(
                    ret[hd], cols(COL_RG, hd)).astype(bf16)
            first_idx = seq_start.astype(jnp.int32) if c == 0 else 0
            probs = [_attention_probs(g, s_t[g], bias_ref, first_idx, sinks_ref, tri_t)
                     for g in groups]
            yield "softmax"
            pv = [_attention_pv(g, probs[g][0], cur, prev) for g in groups]
            yield "pv"
            for g in groups:
                for p in range(2):
                    i = g * 2 + p
                    heads = (2 * p, 2 * p + 1)
                    pair_t = jnp.concatenate(
                        [pv[g][:, hh * CHUNK:(hh + 1) * CHUNK] * probs[g][1][hh] for hh in heads], axis=0)
                    att = pair_t.T
                    mixed_ref[rows, RET_WIDTH + i * PAIR:RET_WIDTH + (i + 1) * PAIR] = (
                        att * _silu(cols(COL_AG, i))).astype(bf16)
            prev = cur
            yield "gate"
        for i in range(N_CARRY):
            carry_ref[i] = prev[i]

        rp = tm // OUT_ROW_PARTS
        for r in range(0, tm, rp):
            out = _dot(mixed_ref[r:r + rp, :], wout_ref[...])
            ms2 = jnp.mean(out * out, axis=-1, keepdims=True)
            o_ref[r:r + rp, :] = xp_ref[r:r + rp, :] + out * lax.rsqrt(ms2 + EPS) * gpost_ref[...]
            yield "out"

    def step(h_cur, h_next, proj_w, proj_r):
        pieces = project(h_cur, proj_w)
        for _ in range(PIECES_AFTER["start"]):
            next(pieces, None)
        for i, phase in enumerate(mix(proj_r)):
            for _ in range(PIECES_AFTER.get(phase, 0)):
                next(pieces, None)
            if i == NORMALISE_AFTER_PHASE:
                normalise(xnext_ref, h_next)
        for _ in pieces:
            pass

    w_blocks = ([(win_hbm, win_ref, j) for j in range(0, IN_COLS, PROJ_COLS)]
                + [(wout_hbm, wout_ref, j) for j in range(0, D_MODEL, PROJ_COLS)])

    def weight_copy(k):
        src, _, j = w_blocks[k]
        slot = k % N_STAGE
        return pltpu.make_async_copy(src.at[:, pl.ds(j, PROJ_COLS)], stage_ref.at[slot],
                                     stage_sems.at[slot])

    def first_step():
        for k in range(N_STAGE):
            weight_copy(k).start()
        normalise(xp_ref, h_a)
        pieces = project(h_a, proj_a)
        for k, (_, dst, j) in enumerate(w_blocks):
            weight_copy(k).wait()
            dst[:, j:j + PROJ_COLS] = stage_ref[k % N_STAGE].astype(bf16)
            if k + N_STAGE < len(w_blocks):
                weight_copy(k + N_STAGE).start()
            if k >= 1:
                next(pieces, None)
        for _ in pieces:
            pass
        normalise(xnext_ref, h_b)

    pl.when(first)(first_step)

    @pl.when(last)
    def _():
        for _ in mix(proj_b if n_tiles % 2 == 0 else proj_a):
            pass

    @pl.when(even & jnp.logical_not(first | last))
    def _():
        step(h_a, h_b, proj_a, proj_b)

    @pl.when(jnp.logical_not(even | last))
    def _():
        step(h_b, h_a, proj_b, proj_a)


def _retention_tables():
    f32 = np.float32
    c = CHUNK
    log_gamma = np.log1p(-np.exp2(-5.0 - np.arange(RET_HEADS, dtype=f32))).astype(f32)
    pos = np.arange(c, dtype=f32)
    diff = pos[:, None] - pos[None, :]
    intra = np.where(diff >= 0, np.exp(log_gamma[:, None, None] * np.maximum(diff, f32(0))), f32(0))
    decay = (intra * f32(RET_DK ** -0.5)).astype(f32)
    xi = np.exp(log_gamma[:, None] * (pos + f32(1)))
    zeta = np.exp(log_gamma[:, None] * (f32(c - 1) - pos)) * f32(RET_DK ** -0.5)
    xi_b = np.broadcast_to(xi[:, :, None], (RET_HEADS, c, RET_DK)).astype(f32)
    zeta_t = np.broadcast_to(zeta[:, None, :], (RET_HEADS, RET_DK, c)).astype(f32)
    chunk_decay = np.exp(log_gamma * f32(c)).astype(f32)
    return decay, xi_b, zeta_t, chunk_decay


def _alibi_bias():
    f32 = np.float32
    t = CHUNK
    i = np.arange(t)[:, None]
    j = np.arange(t)[None, :]
    dist = np.where(j <= i, i - j, i + t - j).astype(f32)
    slopes = np.exp2(-8.0 * (np.arange(ATT_HEADS, dtype=f32) + f32(1)) / f32(ATT_HEADS)).astype(f32)
    bias = (f32(LOG2E) * slopes[:, None, None] * dist[None]).astype(f32)
    first = np.where((j <= i)[None], bias, f32(BIG)).astype(f32)
    return np.ascontiguousarray(np.swapaxes(np.stack([bias, first]), -1, -2))


def kernel(x, g_pre, w_in, sinks, w_out, g_post):
    bsz, seq, d_model = x.shape
    assert d_model == D_MODEL and w_in.shape == (D_MODEL, IN_COLS)
    assert w_out.shape == (D_MIX, D_MODEL) and D_MIX == D_MODEL and seq % TM == 0
    f32, bf16 = jnp.float32, jnp.bfloat16
    decay, xi_b, zeta_t, chunk_decay = _retention_tables()
    bias = _alibi_bias()
    n_tiles = bsz * seq // TM
    x2 = x.reshape(bsz * seq, D_MODEL)

    def const(shape):
        return pl.BlockSpec(shape, lambda n: (0,) * len(shape))

    smem = pl.BlockSpec(memory_space=pltpu.SMEM)
    out = pl.pallas_call(
        functools.partial(_layer_kernel, tm=TM, tiles_per_seq=seq // TM, n_tiles=n_tiles),
        grid=(n_tiles + 1,),
        in_specs=[
            smem,
            smem,
            pl.BlockSpec((TM, D_MODEL), lambda n: (jnp.minimum(n + 1, n_tiles - 1), 0)),
            pl.BlockSpec((TM, D_MODEL), lambda n: (jnp.maximum(n - 1, 0), 0)),
            const((1, D_MODEL)),
            pl.BlockSpec(memory_space=pl.ANY),
            pl.BlockSpec(memory_space=pl.ANY),
            const((1, D_MODEL)),
            const((RET_HEADS, CHUNK, CHUNK)),
            const((RET_HEADS, CHUNK, RET_DK)),
            const((RET_HEADS, RET_DK, CHUNK)),
            const((2, ATT_HEADS, CHUNK, CHUNK)),
        ],
        out_specs=pl.BlockSpec((TM, D_MODEL), lambda n: (jnp.maximum(n - 1, 0), 0)),
        out_shape=jax.ShapeDtypeStruct(x2.shape, x.dtype),
        scratch_shapes=[
            pltpu.VMEM((TM, IN_COLS), f32),
            pltpu.VMEM((TM, IN_COLS), f32),
            pltpu.VMEM((TM, D_MODEL), bf16),
            pltpu.VMEM((TM, D_MODEL), bf16),
            pltpu.VMEM((RET_HEADS, RET_DK, RET_DV), f32),
            pltpu.VMEM((N_CARRY, CHUNK, CHUNK), bf16),
            pltpu.VMEM((TM, D_MIX), bf16),
            pltpu.VMEM((D_MODEL, IN_COLS), bf16),
            pltpu.VMEM((D_MIX, D_MODEL), bf16),
            pltpu.VMEM((N_STAGE, D_MODEL, PROJ_COLS), f32),
            pltpu.SemaphoreType.DMA((N_STAGE,)),
        ],
        compiler_params=pltpu.CompilerParams(
            dimension_semantics=("arbitrary",),
            vmem_limit_bytes=VMEM_LIMIT_BYTES),
        name="hybrid_layer",
    )(sinks.astype(f32), chunk_decay, x2, x2, g_pre.reshape(1, D_MODEL).astype(f32),
      w_in.astype(f32), w_out.astype(f32), g_post.reshape(1, D_MODEL).astype(f32),
      decay, xi_b, zeta_t, bias)
    return out.reshape(x.shape)
```

```python
import functools

import jax
import jax.numpy as jnp
import numpy as np
from jax import lax
from jax.experimental import pallas as pl
from jax.experimental.pallas import tpu as pltpu

D_MODEL = 1024
RET_HEADS = 4
RET_DK = 128
RET_DV = 128
RET_WIDTH = RET_HEADS * RET_DV
CHUNK = 128
ATT_HEADS = 8
ATT_KV_HEADS = 2
ATT_GROUP = ATT_HEADS // ATT_KV_HEADS
ATT_HEAD_DIM = 64
ATT_WIDTH = ATT_HEADS * ATT_HEAD_DIM
ATT_KV_WIDTH = ATT_KV_HEADS * ATT_HEAD_DIM
D_MIX = RET_WIDTH + ATT_WIDTH
EPS = 1e-6
BIG = 1e30
LOG2E = 1.4426950408889634

COL_RQ = 0
COL_RK = COL_RQ + RET_HEADS * RET_DK
COL_RV = COL_RK + RET_HEADS * RET_DK
COL_RG = COL_RV + RET_WIDTH
COL_AQ = COL_RG + RET_WIDTH
COL_AK = COL_AQ + ATT_WIDTH
COL_AV = COL_AK + ATT_KV_WIDTH
COL_AG = COL_AV + ATT_KV_WIDTH
IN_COLS = COL_AG + ATT_WIDTH

PAIR = 2 * ATT_HEAD_DIM
assert PAIR == CHUNK and ATT_KV_WIDTH == CHUNK and RET_DK == CHUNK and RET_DV == CHUNK

TM = 512
VMEM_LIMIT_BYTES = 56 * 1024 * 1024
N_CARRY = 2
PROJ_COLS = 256
STAGE_COLS = 512
N_STAGE = 4
OUT_ROW_PARTS = 2
PIECES_AFTER = {"start": 1, "softmax": 1, "gate": 1, "out": 2}
NORMALISE_AFTER_PHASE = 4
assert IN_COLS % PROJ_COLS == 0


def _silu(x):
    hx = 0.5 * x
    return hx + hx * jnp.tanh(hx)


_dot = functools.partial(jnp.dot, preferred_element_type=jnp.float32)


def _retention_pair_scores(q_pair, k_pair):
    bf16 = jnp.bfloat16
    k_ts = (k_pair[:, :RET_DK].T, k_pair[:, RET_DK:].T)
    zeros = jnp.zeros((RET_DK, CHUNK), bf16)
    rhs = jnp.concatenate([jnp.concatenate([k_ts[0].astype(bf16), zeros], axis=1),
                           jnp.concatenate([zeros, k_ts[1].astype(bf16)], axis=1)], axis=0)
    return _dot(q_pair.astype(bf16), rhs), k_ts


def _retention_mix(sc, q, v, st, decay, xi):
    bf16 = jnp.bfloat16
    lhs = jnp.concatenate([(sc * decay).astype(bf16), (q * xi).astype(bf16)], axis=1)
    rhs = jnp.concatenate([v.astype(bf16), st.astype(bf16)], axis=0)
    return _dot(lhs, rhs)


def _retention_pair_kv(k_ts, v_pair, zeta_a, zeta_b):
    bf16 = jnp.bfloat16
    lhs = jnp.concatenate([(k_ts[0] * zeta_a).astype(bf16), (k_ts[1] * zeta_b).astype(bf16)], axis=1)
    vb = v_pair.astype(bf16)
    zeros = jnp.zeros((CHUNK, RET_DV), bf16)
    rhs = jnp.concatenate([jnp.concatenate([vb[:, :RET_DV], zeros], axis=1),
                           jnp.concatenate([zeros, vb[:, RET_DV:]], axis=1)], axis=0)
    return _dot(lhs, rhs)


def _retention_norm_gate(o, gate):
    mu = jnp.mean(o, axis=-1, keepdims=True)
    d = o - mu
    var = jnp.mean(d * d, axis=-1, keepdims=True)
    return d * lax.rsqrt(var + EPS) * _silu(gate)


def _attention_prep(k_both, v_both):
    bf16 = jnp.bfloat16
    return (k_both * (ATT_HEAD_DIM ** -0.5 * LOG2E)).astype(bf16), v_both.T.astype(bf16)


def _attention_scores(g, q_g, cur, prev):
    bf16 = jnp.bfloat16
    k_win = jnp.concatenate([cur[0], prev[0]], axis=0)
    zeros = jnp.zeros((ATT_HEAD_DIM, CHUNK), bf16)
    cols = []
    for p in range(2):
        q_t = q_g[:, p * PAIR:(p + 1) * PAIR].T.astype(bf16)
        for e in range(2):
            q_h = q_t[e * ATT_HEAD_DIM:(e + 1) * ATT_HEAD_DIM]
            cols.append(jnp.concatenate([q_h, zeros] if g == 0 else [zeros, q_h], axis=0))
    return _dot(k_win, jnp.concatenate(cols, axis=1))


def _attention_probs(g, s_t, bias_t_ref, first_idx, sinks_ref, tri_t):
    bf16 = jnp.bfloat16
    parts, inv_dens = [], []
    for hh in range(ATT_GROUP):
        hq = g * ATT_GROUP + hh
        s = s_t[:, hh * CHUNK:(hh + 1) * CHUNK]
        sf = jnp.where(tri_t, s[:CHUNK], s[CHUNK:]) - bias_t_ref[first_idx, hq]
        m = jnp.max(sf, axis=0, keepdims=True)
        ex = jnp.exp2(sf - m)
        den = jnp.sum(ex, axis=0, keepdims=True) + jnp.exp2(sinks_ref[hq] * LOG2E - m)
        inv_dens.append(1.0 / den)
        parts.append(jnp.concatenate([jnp.where(tri_t, ex, 0.0).astype(bf16),
                                      jnp.where(tri_t, 0.0, ex).astype(bf16)], axis=0))
    return jnp.concatenate(parts, axis=1), inv_dens


def _attention_pv(g, probs_t, cur, prev):
    out = _dot(jnp.concatenate([cur[1], prev[1]], axis=1), probs_t)
    return out[g * ATT_HEAD_DIM:(g + 1) * ATT_HEAD_DIM]


def _layer_kernel(sinks_ref, cdecay_ref, xnext_ref, xp_ref, gpre_ref, win_hbm, wout_hbm,
                  gpost_ref, decay_ref, xi_ref, zeta_t_ref, bias_ref,
                  o_ref, proj_a, proj_b, h_a, h_b, state_ref, carry_ref, mixed_ref,
                  win_ref, wout_ref, stage_ref, stage_sems, *, tm, tiles_per_seq, n_tiles):
    f32, bf16 = jnp.float32, jnp.bfloat16
    n = pl.program_id(0)
    first, last, even = n == 0, n == n_tiles, n % 2 == 0
    seq_start = (jnp.maximum(n - 1, 0) % tiles_per_seq) == 0

    def normalise(x_ref, h_ref):
        x = x_ref[...]
        ms = jnp.mean(x * x, axis=-1, keepdims=True)
        h_ref[...] = (x * lax.rsqrt(ms + EPS) * gpre_ref[...]).astype(bf16)

    @pl.when(seq_start)
    def _():
        state_ref[...] = jnp.zeros_like(state_ref)
        carry_ref[...] = jnp.zeros_like(carry_ref)

    row = lax.broadcasted_iota(jnp.int32, (CHUNK, CHUNK), 0)
    lane = lax.broadcasted_iota(jnp.int32, (CHUNK, CHUNK), 1)
    tri_t = row <= lane

    def project(h_ref, proj_w, row_parts=1):
        rp = tm // row_parts
        for j in range(0, IN_COLS, PROJ_COLS):
            for r in range(0, tm, rp):
                proj_w[r:r + rp, j:j + PROJ_COLS] = _dot(h_ref[r:r + rp, :], win_ref[:, j:j + PROJ_COLS])
            yield

    def mix(proj_r):
        prev = tuple(carry_ref[i] for i in range(N_CARRY))
        pairs, groups = range(RET_HEADS // 2), range(ATT_KV_HEADS)
        for c in range(tm // CHUNK):
            rows = slice(c * CHUNK, (c + 1) * CHUNK)

            def cols(base, i, width=CHUNK):
                return proj_r[rows, base + i * width:base + (i + 1) * width]

            scored = [_retention_pair_scores(cols(COL_RQ, hp, 2 * RET_DK), cols(COL_RK, hp, 2 * RET_DK))
                      for hp in pairs]
            cur = _attention_prep(cols(COL_AK, 0), cols(COL_AV, 0))
            s_t = [_attention_scores(g, cols(COL_AQ, g, 2 * PAIR), cur, prev) for g in groups]
            yield "scores"
            ret = []
            for hd in range(RET_HEADS):
                sc = scored[hd // 2][0][:, (hd % 2) * CHUNK:(hd % 2 + 1) * CHUNK]
                ret.append(_retention_mix(sc, cols(COL_RQ, hd), cols(COL_RV, hd), state_ref[hd],
                                          decay_ref[hd], xi_ref[hd]))
            for hp in pairs:
                kv = _retention_pair_kv(scored[hp][1], cols(COL_RV, hp, 2 * RET_DV),
                                        zeta_t_ref[2 * hp], zeta_t_ref[2 * hp + 1])
                for e in range(2):
                    hd = 2 * hp + e
                    state_ref[hd] = cdecay_ref[hd] * state_ref[hd] + kv[:, e * RET_DV:(e + 1) * RET_DV]
            yield "mix"
            for hd in range(RET_HEADS):
                mixed_ref[rows, hd * RET_DV:(hd + 1) * RET_DV] = _retention_norm_gate(
                    ret[hd], cols(COL_RG, hd)).astype(bf16)
            first_idx = seq_start.astype(jnp.int32) if c == 0 else 0
            probs = [_attention_probs(g, s_t[g], bias_ref, first_idx, sinks_ref, tri_t)
                     for g in groups]
            yield "softmax"
            pv = [_attention_pv(g, probs[g][0], cur, prev) for g in groups]
            yield "pv"
            for g in groups:
                for p in range(2):
                    i = g * 2 + p
                    heads = (2 * p, 2 * p + 1)
                    pair_t = jnp.concatenate(
                        [pv[g][:, hh * CHUNK:(hh + 1) * CHUNK] * probs[g][1][hh] for hh in heads], axis=0)
                    att = pair_t.T
                    mixed_ref[rows, RET_WIDTH + i * PAIR:RET_WIDTH + (i + 1) * PAIR] = (
                        att * _silu(cols(COL_AG, i))).astype(bf16)
            prev = cur
            yield "gate"
        for i in range(N_CARRY):
            carry_ref[i] = prev[i]

        rp = tm // OUT_ROW_PARTS
        for r in range(0, tm, rp):
            out = _dot(mixed_ref[r:r + rp, :], wout_ref[...])
            ms2 = jnp.mean(out * out, axis=-1, keepdims=True)
            o_ref[r:r + rp, :] = xp_ref[r:r + rp, :] + out * lax.rsqrt(ms2 + EPS) * gpost_ref[...]
            yield "out"

    def step(h_cur, h_next, proj_w, proj_r):
        pieces = project(h_cur, proj_w)
        for _ in range(PIECES_AFTER["start"]):
            next(pieces, None)
        for i, phase in enumerate(mix(proj_r)):
            for _ in range(PIECES_AFTER.get(phase, 0)):
                next(pieces, None)
            if i == NORMALISE_AFTER_PHASE:
                normalise(xnext_ref, h_next)
        for _ in pieces:
            pass

    w_blocks = ([(win_hbm, win_ref, j, min(STAGE_COLS, IN_COLS - j)) for j in range(0, IN_COLS, STAGE_COLS)]
                + [(wout_hbm, wout_ref, j, STAGE_COLS) for j in range(0, D_MODEL, STAGE_COLS)])

    def weight_copy(k):
        src, _, j, w = w_blocks[k]
        slot = k % N_STAGE
        return pltpu.make_async_copy(src.at[:, pl.ds(j, w)], stage_ref.at[slot, :, pl.ds(0, w)],
                                     stage_sems.at[slot])

    def first_step():
        for k in range(N_STAGE):
            weight_copy(k).start()
        normalise(xp_ref, h_a)
        pieces = project(h_a, proj_a, row_parts=2)
        for k, (_, dst, j, w) in enumerate(w_blocks):
            weight_copy(k).wait()
            dst[:, j:j + w] = stage_ref[k % N_STAGE, :, 0:w].astype(bf16)
            if k + N_STAGE < len(w_blocks):
                weight_copy(k + N_STAGE).start()
            if k >= 1:
                for _ in range(w_blocks[k - 1][3] // PROJ_COLS):
                    next(pieces, None)
        for _ in pieces:
            pass
        normalise(xnext_ref, h_b)

    pl.when(first)(first_step)

    @pl.when(last)
    def _():
        for _ in mix(proj_b if n_tiles % 2 == 0 else proj_a):
            pass

    @pl.when(even & jnp.logical_not(first | last))
    def _():
        step(h_a, h_b, proj_a, proj_b)

    @pl.when(jnp.logical_not(even | last))
    def _():
        step(h_b, h_a, proj_b, proj_a)


def _retention_tables():
    f32 = np.float32
    c = CHUNK
    log_gamma = np.log1p(-np.exp2(-5.0 - np.arange(RET_HEADS, dtype=f32))).astype(f32)
    pos = np.arange(c, dtype=f32)
    diff = pos[:, None] - pos[None, :]
    intra = np.where(diff >= 0, np.exp(log_gamma[:, None, None] * np.maximum(diff, f32(0))), f32(0))
    decay = (intra * f32(RET_DK ** -0.5)).astype(f32)
    xi = np.exp(log_gamma[:, None] * (pos + f32(1)))
    zeta = np.exp(log_gamma[:, None] * (f32(c - 1) - pos)) * f32(RET_DK ** -0.5)
    xi_b = np.broadcast_to(xi[:, :, None], (RET_HEADS, c, RET_DK)).astype(f32)
    zeta_t = np.broadcast_to(zeta[:, None, :], (RET_HEADS, RET_DK, c)).astype(f32)
    chunk_decay = np.exp(log_gamma * f32(c)).astype(f32)
    return decay, xi_b, zeta_t, chunk_decay


def _alibi_bias():
    f32 = np.float32
    t = CHUNK
    i = np.arange(t)[:, None]
    j = np.arange(t)[None, :]
    dist = np.where(j <= i, i - j, i + t - j).astype(f32)
    slopes = np.exp2(-8.0 * (np.arange(ATT_HEADS, dtype=f32) + f32(1)) / f32(ATT_HEADS)).astype(f32)
    bias = (f32(LOG2E) * slopes[:, None, None] * dist[None]).astype(f32)
    first = np.where((j <= i)[None], bias, f32(BIG)).astype(f32)
    return np.ascontiguousarray(np.swapaxes(np.stack([bias, first]), -1, -2))


def kernel(x, g_pre, w_in, sinks, w_out, g_post):
    bsz, seq, d_model = x.shape
    assert d_model == D_MODEL and w_in.shape == (D_MODEL, IN_COLS)
    assert w_out.shape == (D_MIX, D_MODEL) and D_MIX == D_MODEL and seq % TM == 0
    f32, bf16 = jnp.float32, jnp.bfloat16
    decay, xi_b, zeta_t, chunk_decay = _retention_tables()
    bias = _alibi_bias()
    n_tiles = bsz * seq // TM
    x2 = x.reshape(bsz * seq, D_MODEL)

    def const(shape):
        return pl.BlockSpec(shape, lambda n: (0,) * len(shape))

    smem = pl.BlockSpec(memory_space=pltpu.SMEM)
    out = pl.pallas_call(
        functools.partial(_layer_kernel, tm=TM, tiles_per_seq=seq // TM, n_tiles=n_tiles),
        grid=(n_tiles + 1,),
        in_specs=[
            smem,
            smem,
            pl.BlockSpec((TM, D_MODEL), lambda n: (jnp.minimum(n + 1, n_tiles - 1), 0)),
            pl.BlockSpec((TM, D_MODEL), lambda n: (jnp.maximum(n - 1, 0), 0)),
            const((1, D_MODEL)),
            pl.BlockSpec(memory_space=pl.ANY),
            pl.BlockSpec(memory_space=pl.ANY),
            const((1, D_MODEL)),
            const((RET_HEADS, CHUNK, CHUNK)),
            const((RET_HEADS, CHUNK, RET_DK)),
            const((RET_HEADS, RET_DK, CHUNK)),
            const((2, ATT_HEADS, CHUNK, CHUNK)),
        ],
        out_specs=pl.BlockSpec((TM, D_MODEL), lambda n: (jnp.maximum(n - 1, 0), 0)),
        out_shape=jax.ShapeDtypeStruct(x2.shape, x.dtype),
        scratch_shapes=[
            pltpu.VMEM((TM, IN_COLS), f32),
            pltpu.VMEM((TM, IN_COLS), f32),
            pltpu.VMEM((TM, D_MODEL), bf16),
            pltpu.VMEM((TM, D_MODEL), bf16),
            pltpu.VMEM((RET_HEADS, RET_DK, RET_DV), f32),
            pltpu.VMEM((N_CARRY, CHUNK, CHUNK), bf16),
            pltpu.VMEM((TM, D_MIX), bf16),
            pltpu.VMEM((D_MODEL, IN_COLS), bf16),
            pltpu.VMEM((D_MIX, D_MODEL), bf16),
            pltpu.VMEM((N_STAGE, D_MODEL, STAGE_COLS), f32),
            pltpu.SemaphoreType.DMA((N_STAGE,)),
        ],
        compiler_params=pltpu.CompilerParams(
            dimension_semantics=("arbitrary",),
            vmem_limit_bytes=VMEM_LIMIT_BYTES),
        name="hybrid_layer",
    )(sinks.astype(f32), chunk_decay, x2, x2, g_pre.reshape(1, D_MODEL).astype(f32),
      w_in.astype(f32), w_out.astype(f32), g_post.reshape(1, D_MODEL).astype(f32),
      decay, xi_b, zeta_t, bias)
    return out.reshape(x.shape)
```

```python
import functools

import jax
import jax.numpy as jnp
import numpy as np
from jax import lax
from jax.experimental import pallas as pl
from jax.experimental.pallas import tpu as pltpu

D_MODEL = 1024
RET_HEADS = 4
RET_DK = 128
RET_DV = 128
RET_WIDTH = RET_HEADS * RET_DV
CHUNK = 128
ATT_HEADS = 8
ATT_KV_HEADS = 2
ATT_GROUP = ATT_HEADS // ATT_KV_HEADS
ATT_HEAD_DIM = 64
ATT_WIDTH = ATT_HEADS * ATT_HEAD_DIM
ATT_KV_WIDTH = ATT_KV_HEADS * ATT_HEAD_DIM
D_MIX = RET_WIDTH + ATT_WIDTH
EPS = 1e-6
BIG = 1e30
LOG2E = 1.4426950408889634

COL_RQ = 0
COL_RK = COL_RQ + RET_HEADS * RET_DK
COL_RV = COL_RK + RET_HEADS * RET_DK
COL_RG = COL_RV + RET_WIDTH
COL_AQ = COL_RG + RET_WIDTH
COL_AK = COL_AQ + ATT_WIDTH
COL_AV = COL_AK + ATT_KV_WIDTH
COL_AG = COL_AV + ATT_KV_WIDTH
IN_COLS = COL_AG + ATT_WIDTH

PAIR = 2 * ATT_HEAD_DIM
assert PAIR == CHUNK and ATT_KV_WIDTH == CHUNK and RET_DK == CHUNK and RET_DV == CHUNK

TM = 512
VMEM_LIMIT_BYTES = 56 * 1024 * 1024
N_CARRY = 2
PROJ_COLS = 256
STAGE_COLS = 512
N_STAGE = 4
OUT_ROW_PARTS = 2
PIECES_AFTER = {"start": 1, "softmax": 1, "gate": 1, "out": 2}
NORMALISE_AFTER_PHASE = 4
assert IN_COLS % PROJ_COLS == 0


def _silu(x):
    hx = 0.5 * x
    return hx + hx * jnp.tanh(hx)


_dot = functools.partial(jnp.dot, preferred_element_type=jnp.float32)


def _retention_pair_scores(q_pair, k_pair):
    bf16 = jnp.bfloat16
    k_ts = (k_pair[:, :RET_DK].T, k_pair[:, RET_DK:].T)
    zeros = jnp.zeros((RET_DK, CHUNK), bf16)
    rhs = jnp.concatenate([jnp.concatenate([k_ts[0].astype(bf16), zeros], axis=1),
                           jnp.concatenate([zeros, k_ts[1].astype(bf16)], axis=1)], axis=0)
    return _dot(q_pair.astype(bf16), rhs), k_ts


def _retention_mix(sc, q, v, st, decay, xi):
    bf16 = jnp.bfloat16
    lhs = jnp.concatenate([(sc * decay).astype(bf16), (q * xi).astype(bf16)], axis=1)
    rhs = jnp.concatenate([v.astype(bf16), st.astype(bf16)], axis=0)
    return _dot(lhs, rhs)


def _retention_pair_kv(k_ts, v_pair, zeta_a, zeta_b):
    bf16 = jnp.bfloat16
    lhs = jnp.concatenate([(k_ts[0] * zeta_a).astype(bf16), (k_ts[1] * zeta_b).astype(bf16)], axis=1)
    vb = v_pair.astype(bf16)
    zeros = jnp.zeros((CHUNK, RET_DV), bf16)
    rhs = jnp.concatenate([jnp.concatenate([vb[:, :RET_DV], zeros], axis=1),
                           jnp.concatenate([zeros, vb[:, RET_DV:]], axis=1)], axis=0)
    return _dot(lhs, rhs)


def _retention_norm_gate(o, gate):
    mu = jnp.mean(o, axis=-1, keepdims=True)
    d = o - mu
    var = jnp.mean(d * d, axis=-1, keepdims=True)
    return d * lax.rsqrt(var + EPS) * _silu(gate)


def _attention_prep(k_both, v_both):
    bf16 = jnp.bfloat16
    return (k_both * (ATT_HEAD_DIM ** -0.5 * LOG2E)).astype(bf16), v_both.T.astype(bf16)


def _attention_scores(g, q_g, cur, prev):
    bf16 = jnp.bfloat16
    k_win = jnp.concatenate([cur[0], prev[0]], axis=0)
    zeros = jnp.zeros((ATT_HEAD_DIM, CHUNK), bf16)
    cols = []
    for p in range(2):
        q_t = q_g[:, p * PAIR:(p + 1) * PAIR].T.astype(bf16)
        for e in range(2):
            q_h = q_t[e * ATT_HEAD_DIM:(e + 1) * ATT_HEAD_DIM]
            cols.append(jnp.concatenate([q_h, zeros] if g == 0 else [zeros, q_h], axis=0))
    return _dot(k_win, jnp.concatenate(cols, axis=1))


def _attention_probs(g, s_t, bias_t_ref, first_idx, sinks_ref, tri_t):
    bf16 = jnp.bfloat16
    parts, inv_dens = [], []
    for hh in range(ATT_GROUP):
        hq = g * ATT_GROUP + hh
        s = s_t[:, hh * CHUNK:(hh + 1) * CHUNK]
        sf = jnp.where(tri_t, s[:CHUNK], s[CHUNK:]) - bias_t_ref[first_idx, hq]
        m = jnp.max(sf, axis=0, keepdims=True)
        ex = jnp.exp2(sf - m)
        den = jnp.sum(ex, axis=0, keepdims=True) + jnp.exp2(sinks_ref[hq] * LOG2E - m)
        inv_dens.append(1.0 / den)
        parts.append(jnp.concatenate([jnp.where(tri_t, ex, 0.0).astype(bf16),
                                      jnp.where(tri_t, 0.0, ex).astype(bf16)], axis=0))
    return jnp.concatenate(parts, axis=1), inv_dens


def _attention_pv(g, probs_t, cur, prev):
    out = _dot(jnp.concatenate([cur[1], prev[1]], axis=1), probs_t)
    return out[g * ATT_HEAD_DIM:(g + 1) * ATT_HEAD_DIM]


def _layer_kernel(sinks_ref, cdecay_ref, xnext_ref, xp_ref, gpre_ref, win_hbm, wout_hbm,
                  gpost_ref, decay_ref, xi_ref, zeta_t_ref, bias_ref,
                  o_ref, proj_a, proj_b, h_a, h_b, state_ref, carry_ref, mixed_ref,
                  win_ref, wout_ref, stage_ref, stage_sems, *, tm, tiles_per_seq, n_tiles):
    f32, bf16 = jnp.float32, jnp.bfloat16
    n = pl.program_id(0)
    first, last, even = n == 0, n == n_tiles, n % 2 == 0
    seq_start = (jnp.maximum(n - 1, 0) % tiles_per_seq) == 0

    def normalise(x_ref, h_ref):
        x = x_ref[...]
        ms = jnp.mean(x * x, axis=-1, keepdims=True)
        h_ref[...] = (x * lax.rsqrt(ms + EPS) * gpre_ref[...]).astype(bf16)

    @pl.when(seq_start)
    def _():
        state_ref[...] = jnp.zeros_like(state_ref)
        carry_ref[...] = jnp.zeros_like(carry_ref)

    row = lax.broadcasted_iota(jnp.int32, (CHUNK, CHUNK), 0)
    lane = lax.broadcasted_iota(jnp.int32, (CHUNK, CHUNK), 1)
    tri_t = row <= lane

    def project(h_ref, proj_w, row_parts=1):
        rp = tm // row_parts
        for j in range(0, IN_COLS, PROJ_COLS):
            for r in range(0, tm, rp):
                proj_w[r:r + rp, j:j + PROJ_COLS] = _dot(h_ref[r:r + rp, :], win_ref[:, j:j + PROJ_COLS])
            yield

    def mix(proj_r):
        prev = tuple(carry_ref[i] for i in range(N_CARRY))
        pairs, groups = range(RET_HEADS // 2), range(ATT_KV_HEADS)
        for c in range(tm // CHUNK):
            rows = slice(c * CHUNK, (c + 1) * CHUNK)

            def cols(base, i, width=CHUNK):
                return proj_r[rows, base + i * width:base + (i + 1) * width]

            scored = [_retention_pair_scores(cols(COL_RQ, hp, 2 * RET_DK), cols(COL_RK, hp, 2 * RET_DK))
                      for hp in pairs]
            cur = _attention_prep(cols(COL_AK, 0), cols(COL_AV, 0))
            s_t = [_attention_scores(g, cols(COL_AQ, g, 2 * PAIR), cur, prev) for g in groups]
            yield "scores"
            ret = []
            for hd in range(RET_HEADS):
                sc = scored[hd // 2][0][:, (hd % 2) * CHUNK:(hd % 2 + 1) * CHUNK]
                ret.append(_retention_mix(sc, cols(COL_RQ, hd), cols(COL_RV, hd), state_ref[hd],
                                          decay_ref[hd], xi_ref[hd]))
            for hp in pairs:
                kv = _retention_pair_kv(scored[hp][1], cols(COL_RV, hp, 2 * RET_DV),
                                        zeta_t_ref[2 * hp], zeta_t_ref[2 * hp + 1])
                for e in range(2):
                    hd = 2 * hp + e
                    state_ref[hd] = cdecay_ref[hd] * state_ref[hd] + kv[:, e * RET_DV:(e + 1) * RET_DV]
            yield "mix"
            for hd in range(RET_HEADS):
                mixed_ref[rows, hd * RET_DV:(hd + 1) * RET_DV] = _retention_norm_gate(
                    ret[hd], cols(COL_RG, hd)).astype(bf16)
            first_idx = seq_start.astype(jnp.int32) if c == 0 else 0
            probs = [_attention_probs(g, s_t[g], bias_ref, first_idx, sinks_ref, tri_t)
                     for g in groups]
            yield "softmax"
            pv = [_attention_pv(g, probs[g][0], cur, prev) for g in groups]
            yield "pv"
            for g in groups:
                for p in range(2):
                    i = g * 2 + p
                    heads = (2 * p, 2 * p + 1)
                    pair_t = jnp.concatenate(
                        [pv[g][:, hh * CHUNK:(hh + 1) * CHUNK] * probs[g][1][hh] for hh in heads], axis=0)
                    att = pair_t.T
                    mixed_ref[rows, RET_WIDTH + i * PAIR:RET_WIDTH + (i + 1) * PAIR] = (
                        att * _silu(cols(COL_AG, i))).astype(bf16)
            prev = cur
            yield "gate"
        for i in range(N_CARRY):
            carry_ref[i] = prev[i]

        rp = tm // OUT_ROW_PARTS
        for r in range(0, tm, rp):
            out = _dot(mixed_ref[r:r + rp, :], wout_ref[...])
            ms2 = jnp.mean(out * out, axis=-1, keepdims=True)
            o_ref[r:r + rp, :] = xp_ref[r:r + rp, :] + out * lax.rsqrt(ms2 + EPS) * gpost_ref[...]
            yield "out"

    def step(h_cur, h_next, proj_w, proj_r):
        pieces = project(h_cur, proj_w)
        for _ in range(PIECES_AFTER["start"]):
            next(pieces, None)
        for i, phase in enumerate(mix(proj_r)):
            for _ in range(PIECES_AFTER.get(phase, 0)):
                next(pieces, None)
            if i == NORMALISE_AFTER_PHASE:
                normalise(xnext_ref, h_next)
        for _ in pieces:
            pass

    w_blocks = ([(win_hbm, win_ref, j, min(STAGE_COLS, IN_COLS - j)) for j in range(0, IN_COLS, STAGE_COLS)]
                + [(wout_hbm, wout_ref, j, STAGE_COLS) for j in range(0, D_MODEL, STAGE_COLS)])

    def weight_copy(k):
        src, _, j, w = w_blocks[k]
        slot = k % N_STAGE
        return pltpu.make_async_copy(src.at[:, pl.ds(j, w)], stage_ref.at[slot, :, pl.ds(0, w)],
                                     stage_sems.at[slot])

    n_win_blocks = sum(1 for blk in w_blocks if blk[1] is win_ref)
    assert len(w_blocks) - n_win_blocks <= N_STAGE

    def land(k):
        _, dst, j, w = w_blocks[k]
        weight_copy(k).wait()
        dst[:, j:j + w] = stage_ref[k % N_STAGE, :, 0:w].astype(bf16)

    def first_step():
        for k in range(N_STAGE):
            weight_copy(k).start()
        normalise(xp_ref, h_a)
        pieces = project(h_a, proj_a, row_parts=2)
        for k in range(n_win_blocks):
            land(k)
            if k + N_STAGE < len(w_blocks):
                weight_copy(k + N_STAGE).start()
            if k >= 1:
                for _ in range(w_blocks[k - 1][3] // PROJ_COLS):
                    next(pieces, None)
        for _ in pieces:
            pass
        normalise(xnext_ref, h_b)

    @pl.when(n == 1)
    def _():
        for k in range(n_win_blocks, len(w_blocks)):
            land(k)

    pl.when(first)(first_step)

    @pl.when(last)
    def _():
        for _ in mix(proj_b if n_tiles % 2 == 0 else proj_a):
            pass

    @pl.when(even & jnp.logical_not(first | last))
    def _():
        step(h_a, h_b, proj_a, proj_b)

    @pl.when(jnp.logical_not(even | last))
    def _():
        step(h_b, h_a, proj_b, proj_a)


def _retention_tables():
    f32 = np.float32
    c = CHUNK
    log_gamma = np.log1p(-np.exp2(-5.0 - np.arange(RET_HEADS, dtype=f32))).astype(f32)
    pos = np.arange(c, dtype=f32)
    diff = pos[:, None] - pos[None, :]
    intra = np.where(diff >= 0, np.exp(log_gamma[:, None, None] * np.maximum(diff, f32(0))), f32(0))
    decay = (intra * f32(RET_DK ** -0.5)).astype(f32)
    xi = np.exp(log_gamma[:, None] * (pos + f32(1)))
    zeta = np.exp(log_gamma[:, None] * (f32(c - 1) - pos)) * f32(RET_DK ** -0.5)
    xi_b = np.broadcast_to(xi[:, :, None], (RET_HEADS, c, RET_DK)).astype(f32)
    zeta_t = np.broadcast_to(zeta[:, None, :], (RET_HEADS, RET_DK, c)).astype(f32)
    chunk_decay = np.exp(log_gamma * f32(c)).astype(f32)
    return decay, xi_b, zeta_t, chunk_decay


def _alibi_bias():
    f32 = np.float32
    t = CHUNK
    i = np.arange(t)[:, None]
    j = np.arange(t)[None, :]
    dist = np.where(j <= i, i - j, i + t - j).astype(f32)
    slopes = np.exp2(-8.0 * (np.arange(ATT_HEADS, dtype=f32) + f32(1)) / f32(ATT_HEADS)).astype(f32)
    bias = (f32(LOG2E) * slopes[:, None, None] * dist[None]).astype(f32)
    first = np.where((j <= i)[None], bias, f32(BIG)).astype(f32)
    return np.ascontiguousarray(np.swapaxes(np.stack([bias, first]), -1, -2))


def kernel(x, g_pre, w_in, sinks, w_out, g_post):
    bsz, seq, d_model = x.shape
    assert d_model == D_MODEL and w_in.shape == (D_MODEL, IN_COLS)
    assert w_out.shape == (D_MIX, D_MODEL) and D_MIX == D_MODEL and seq % TM == 0
    f32, bf16 = jnp.float32, jnp.bfloat16
    decay, xi_b, zeta_t, chunk_decay = _retention_tables()
    bias = _alibi_bias()
    n_tiles = bsz * seq // TM
    x2 = x.reshape(bsz * seq, D_MODEL)

    def const(shape):
        return pl.BlockSpec(shape, lambda n: (0,) * len(shape))

    smem = pl.BlockSpec(memory_space=pltpu.SMEM)
    out = pl.pallas_call(
        functools.partial(_layer_kernel, tm=TM, tiles_per_seq=seq // TM, n_tiles=n_tiles),
        grid=(n_tiles + 1,),
        in_specs=[
            smem,
            smem,
            pl.BlockSpec((TM, D_MODEL), lambda n: (jnp.minimum(n + 1, n_tiles - 1), 0)),
            pl.BlockSpec((TM, D_MODEL), lambda n: (jnp.maximum(n - 1, 0), 0)),
            const((1, D_MODEL)),
            pl.BlockSpec(memory_space=pl.ANY),
            pl.BlockSpec(memory_space=pl.ANY),
            const((1, D_MODEL)),
            const((RET_HEADS, CHUNK, CHUNK)),
            const((RET_HEADS, CHUNK, RET_DK)),
            const((RET_HEADS, RET_DK, CHUNK)),
            const((2, ATT_HEADS, CHUNK, CHUNK)),
        ],
        out_specs=pl.BlockSpec((TM, D_MODEL), lambda n: (jnp.maximum(n - 1, 0), 0)),
        out_shape=jax.ShapeDtypeStruct(x2.shape, x.dtype),
        scratch_shapes=[
            pltpu.VMEM((TM, IN_COLS), f32),
            pltpu.VMEM((TM, IN_COLS), f32),
            pltpu.VMEM((TM, D_MODEL), bf16),
            pltpu.VMEM((TM, D_MODEL), bf16),
            pltpu.VMEM((RET_HEADS, RET_DK, RET_DV), f32),
            pltpu.VMEM((N_CARRY, CHUNK, CHUNK), bf16),
            pltpu.VMEM((TM, D_MIX), bf16),
            pltpu.VMEM((D_MODEL, IN_COLS), bf16),
            pltpu.VMEM((D_MIX, D_MODEL), bf16),
            pltpu.VMEM((N_STAGE, D_MODEL, STAGE_COLS), f32),
            pltpu.SemaphoreType.DMA((N_STAGE,)),
        ],
        compiler_params=pltpu.CompilerParams(
            dimension_semantics=("arbitrary",),
            vmem_limit_bytes=VMEM_LIMIT_BYTES),
        name="hybrid_layer",
    )(sinks.astype(f32), chunk_decay, x2, x2, g_pre.reshape(1, D_MODEL).astype(f32),
      w_in.astype(f32), w_out.astype(f32), g_post.reshape(1, D_MODEL).astype(f32),
      decay, xi_b, zeta_t, bias)
    return out.reshape(x.shape)
```

```python
import functools

import jax
import jax.numpy as jnp
import numpy as np
from jax import lax
from jax.experimental import pallas as pl
from jax.experimental.pallas import tpu as pltpu

D_MODEL = 1024
RET_HEADS = 4
RET_DK = 128
RET_DV = 128
RET_WIDTH = RET_HEADS * RET_DV
CHUNK = 128
ATT_HEADS = 8
ATT_KV_HEADS = 2
ATT_GROUP = ATT_HEADS // ATT_KV_HEADS
ATT_HEAD_DIM = 64
ATT_WIDTH = ATT_HEADS * ATT_HEAD_DIM
ATT_KV_WIDTH = ATT_KV_HEADS * ATT_HEAD_DIM
D_MIX = RET_WIDTH + ATT_WIDTH
EPS = 1e-6
BIG = 1e30
LOG2E = 1.4426950408889634

COL_RQ = 0
COL_RK = COL_RQ + RET_HEADS * RET_DK
COL_RV = COL_RK + RET_HEADS * RET_DK
COL_RG = COL_RV + RET_WIDTH
COL_AQ = COL_RG + RET_WIDTH
COL_AK = COL_AQ + ATT_WIDTH
COL_AV = COL_AK + ATT_KV_WIDTH
COL_AG = COL_AV + ATT_KV_WIDTH
IN_COLS = COL_AG + ATT_WIDTH

PAIR = 2 * ATT_HEAD_DIM
assert PAIR == CHUNK and ATT_KV_WIDTH == CHUNK and RET_DK == CHUNK and RET_DV == CHUNK

TM = 512
VMEM_LIMIT_BYTES = 56 * 1024 * 1024
N_CARRY = 2
PROJ_COLS = 256
STAGE_COLS = 512
N_STAGE = 4
OUT_ROW_PARTS = 2
PIECES_AFTER = {"start": 1, "softmax": 1, "gate": 1, "out": 2}
NORMALISE_AFTER_PHASE = 4
assert IN_COLS % PROJ_COLS == 0


def _silu(x):
    hx = 0.5 * x
    return hx + hx * jnp.tanh(hx)


_dot = functools.partial(jnp.dot, preferred_element_type=jnp.float32)


def _retention_pair_scores(q_pair, k_pair):
    bf16 = jnp.bfloat16
    k_ts = (k_pair[:, :RET_DK].T, k_pair[:, RET_DK:].T)
    zeros = jnp.zeros((RET_DK, CHUNK), bf16)
    rhs = jnp.concatenate([jnp.concatenate([k_ts[0].astype(bf16), zeros], axis=1),
                           jnp.concatenate([zeros, k_ts[1].astype(bf16)], axis=1)], axis=0)
    return _dot(q_pair.astype(bf16), rhs), k_ts


def _retention_mix(sc, q, v, st, decay, xi):
    bf16 = jnp.bfloat16
    lhs = jnp.concatenate([(sc * decay).astype(bf16), (q * xi).astype(bf16)], axis=1)
    rhs = jnp.concatenate([v.astype(bf16), st.astype(bf16)], axis=0)
    return _dot(lhs, rhs)


def _retention_pair_kv(k_ts, v_pair, zeta_a, zeta_b):
    bf16 = jnp.bfloat16
    lhs = jnp.concatenate([(k_ts[0] * zeta_a).astype(bf16), (k_ts[1] * zeta_b).astype(bf16)], axis=1)
    vb = v_pair.astype(bf16)
    zeros = jnp.zeros((CHUNK, RET_DV), bf16)
    rhs = jnp.concatenate([jnp.concatenate([vb[:, :RET_DV], zeros], axis=1),
                           jnp.concatenate([zeros, vb[:, RET_DV:]], axis=1)], axis=0)
    return _dot(lhs, rhs)


def _retention_norm_gate(o, gate):
    mu = jnp.mean(o, axis=-1, keepdims=True)
    d = o - mu
    var = jnp.mean(d * d, axis=-1, keepdims=True)
    return d * lax.rsqrt(var + EPS) * _silu(gate)


def _attention_prep(k_both, v_both):
    bf16 = jnp.bfloat16
    return (k_both * (ATT_HEAD_DIM ** -0.5 * LOG2E)).astype(bf16), v_both.T.astype(bf16)


def _attention_scores(g, q_g, cur, prev):
    bf16 = jnp.bfloat16
    k_win = jnp.concatenate([cur[0], prev[0]], axis=0)
    zeros = jnp.zeros((ATT_HEAD_DIM, CHUNK), bf16)
    cols = []
    for p in range(2):
        q_t = q_g[:, p * PAIR:(p + 1) * PAIR].T.astype(bf16)
        for e in range(2):
            q_h = q_t[e * ATT_HEAD_DIM:(e + 1) * ATT_HEAD_DIM]
            cols.append(jnp.concatenate([q_h, zeros] if g == 0 else [zeros, q_h], axis=0))
    return _dot(k_win, jnp.concatenate(cols, axis=1))


def _attention_probs(g, s_t, bias_t_ref, first_idx, sinks_ref, tri_t):
    bf16 = jnp.bfloat16
    parts, inv_dens = [], []
    for hh in range(ATT_GROUP):
        hq = g * ATT_GROUP + hh
        s = s_t[:, hh * CHUNK:(hh + 1) * CHUNK]
        sf = jnp.where(tri_t, s[:CHUNK], s[CHUNK:]) - bias_t_ref[first_idx, hq]
        m = jnp.max(sf, axis=0, keepdims=True)
        ex = jnp.exp2(sf - m)
        den = jnp.sum(ex, axis=0, keepdims=True) + jnp.exp2(sinks_ref[hq] * LOG2E - m)
        inv_dens.append(1.0 / den)
        parts.append(jnp.concatenate([jnp.where(tri_t, ex, 0.0).astype(bf16),
                                      jnp.where(tri_t, 0.0, ex).astype(bf16)], axis=0))
    return jnp.concatenate(parts, axis=1), inv_dens


def _attention_pv(g, probs_t, cur, prev):
    out = _dot(jnp.concatenate([cur[1], prev[1]], axis=1), probs_t)
    return out[g * ATT_HEAD_DIM:(g + 1) * ATT_HEAD_DIM]


def _layer_kernel(sinks_ref, cdecay_ref, xnext_ref, xp_ref, gpre_ref, win_hbm, wout_hbm,
                  gpost_ref, decay_ref, xi_ref, zeta_t_ref, bias_ref,
                  o_ref, proj_a, proj_b, h_a, h_b, state_ref, carry_ref, mixed_ref,
                  win_ref, wout_ref, stage_ref, stage_sems, *, tm, tiles_per_seq, n_tiles):
    f32, bf16 = jnp.float32, jnp.bfloat16
    n = pl.program_id(0)
    first, last, even = n == 0, n == n_tiles, n % 2 == 0
    seq_start = (jnp.maximum(n - 1, 0) % tiles_per_seq) == 0

    def normalise(x_ref, h_ref):
        x = x_ref[...]
        ms = jnp.mean(x * x, axis=-1, keepdims=True)
        h_ref[...] = (x * lax.rsqrt(ms + EPS) * gpre_ref[...]).astype(bf16)

    @pl.when(seq_start)
    def _():
        state_ref[...] = jnp.zeros_like(state_ref)
        carry_ref[...] = jnp.zeros_like(carry_ref)

    row = lax.broadcasted_iota(jnp.int32, (CHUNK, CHUNK), 0)
    lane = lax.broadcasted_iota(jnp.int32, (CHUNK, CHUNK), 1)
    tri_t = row <= lane

    def project(h_ref, proj_w, row_parts=1):
        rp = tm // row_parts
        for j in range(0, IN_COLS, PROJ_COLS):
            for r in range(0, tm, rp):
                proj_w[r:r + rp, j:j + PROJ_COLS] = _dot(h_ref[r:r + rp, :], win_ref[:, j:j + PROJ_COLS])
            yield

    def mix(proj_r):
        prev = tuple(carry_ref[i] for i in range(N_CARRY))
        pairs, groups = range(RET_HEADS // 2), range(ATT_KV_HEADS)
        for c in range(tm // CHUNK):
            rows = slice(c * CHUNK, (c + 1) * CHUNK)

            def cols(base, i, width=CHUNK):
                return proj_r[rows, base + i * width:base + (i + 1) * width]

            scored = [_retention_pair_scores(cols(COL_RQ, hp, 2 * RET_DK), cols(COL_RK, hp, 2 * RET_DK))
                      for hp in pairs]
            cur = _attention_prep(cols(COL_AK, 0), cols(COL_AV, 0))
            s_t = [_attention_scores(g, cols(COL_AQ, g, 2 * PAIR), cur, prev) for g in groups]
            yield "scores"
            ret = []
            for hd in range(RET_HEADS):
                sc = scored[hd // 2][0][:, (hd % 2) * CHUNK:(hd % 2 + 1) * CHUNK]
                ret.append(_retention_mix(sc, cols(COL_RQ, hd), cols(COL_RV, hd), state_ref[hd],
                                          decay_ref[hd], xi_ref[hd]))
            for hp in pairs:
                kv = _retention_pair_kv(scored[hp][1], cols(COL_RV, hp, 2 * RET_DV),
                                        zeta_t_ref[2 * hp], zeta_t_ref[2 * hp + 1])
                for e in range(2):
                    hd = 2 * hp + e
                    state_ref[hd] = cdecay_ref[hd] * state_ref[hd] + kv[:, e * RET_DV:(e + 1) * RET_DV]
            yield "mix"
            for hd in range(RET_HEADS):
                mixed_ref[rows, hd * RET_DV:(hd + 1) * RET_DV] = _retention_norm_gate(
                    ret[hd], cols(COL_RG, hd)).astype(bf16)
            first_idx = seq_start.astype(jnp.int32) if c == 0 else 0
            probs = [_attention_probs(g, s_t[g], bias_ref, first_idx, sinks_ref, tri_t)
                     for g in groups]
            yield "softmax"
            pv = [_attention_pv(g, probs[g][0], cur, prev) for g in groups]
            yield "pv"
            for g in groups:
                for p in range(2):
                    i = g * 2 + p
                    heads = (2 * p, 2 * p + 1)
                    pair_t = jnp.concatenate(
                        [pv[g][:, hh * CHUNK:(hh + 1) * CHUNK] * probs[g][1][hh] for hh in heads], axis=0)
                    att = pair_t.T
                    mixed_ref[rows, RET_WIDTH + i * PAIR:RET_WIDTH + (i + 1) * PAIR] = (
                        att * _silu(cols(COL_AG, i))).astype(bf16)
            prev = cur
            yield "gate"
        for i in range(N_CARRY):
            carry_ref[i] = prev[i]

        rp = tm // OUT_ROW_PARTS
        for r in range(0, tm, rp):
            out = _dot(mixed_ref[r:r + rp, :], wout_ref[...])
            ms2 = jnp.mean(out * out, axis=-1, keepdims=True)
            o_ref[r:r + rp, :] = xp_ref[r:r + rp, :] + out * lax.rsqrt(ms2 + EPS) * gpost_ref[...]
            yield "out"

    def step(h_cur, h_next, proj_w, proj_r):
        pieces = project(h_cur, proj_w)
        for _ in range(PIECES_AFTER["start"]):
            next(pieces, None)
        for i, phase in enumerate(mix(proj_r)):
            for _ in range(PIECES_AFTER.get(phase, 0)):
                next(pieces, None)
            if i == NORMALISE_AFTER_PHASE:
                normalise(xnext_ref, h_next)
        for _ in pieces:
            pass

    w_blocks = ([(win_hbm, win_ref, j, min(STAGE_COLS, IN_COLS - j)) for j in range(0, IN_COLS, STAGE_COLS)]
                + [(wout_hbm, wout_ref, j, STAGE_COLS) for j in range(0, D_MODEL, STAGE_COLS)])

    def weight_copy(k):
        src, _, j, w = w_blocks[k]
        slot = k % N_STAGE
        return pltpu.make_async_copy(src.at[:, pl.ds(j, w)], stage_ref.at[slot, :, pl.ds(0, w)],
                                     stage_sems.at[slot])

    n_win_blocks = sum(1 for blk in w_blocks if blk[1] is win_ref)
    assert len(w_blocks) - n_win_blocks <= N_STAGE

    def land(k):
        _, dst, j, w = w_blocks[k]
        weight_copy(k).wait()
        dst[:, j:j + w] = stage_ref[k % N_STAGE, :, 0:w].astype(bf16)

    def first_step():
        for k in range(N_STAGE):
            weight_copy(k).start(priority=k % 2)
        normalise(xp_ref, h_a)
        pieces = project(h_a, proj_a, row_parts=2)
        for k in range(n_win_blocks):
            land(k)
            if k + N_STAGE < len(w_blocks):
                weight_copy(k + N_STAGE).start(priority=k % 2)
            if k >= 1:
                for _ in range(w_blocks[k - 1][3] // PROJ_COLS):
                    next(pieces, None)
        for _ in pieces:
            pass
        normalise(xnext_ref, h_b)

    @pl.when(n == 1)
    def _():
        for k in range(n_win_blocks, len(w_blocks)):
            land(k)

    pl.when(first)(first_step)

    @pl.when(last)
    def _():
        for _ in mix(proj_b if n_tiles % 2 == 0 else proj_a):
            pass

    @pl.when(even & jnp.logical_not(first | last))
    def _():
        step(h_a, h_b, proj_a, proj_b)

    @pl.when(jnp.logical_not(even | last))
    def _():
        step(h_b, h_a, proj_b, proj_a)


def _retention_tables():
    f32 = np.float32
    c = CHUNK
    log_gamma = np.log1p(-np.exp2(-5.0 - np.arange(RET_HEADS, dtype=f32))).astype(f32)
    pos = np.arange(c, dtype=f32)
    diff = pos[:, None] - pos[None, :]
    intra = np.where(diff >= 0, np.exp(log_gamma[:, None, None] * np.maximum(diff, f32(0))), f32(0))
    decay = (intra * f32(RET_DK ** -0.5)).astype(f32)
    xi = np.exp(log_gamma[:, None] * (pos + f32(1)))
    zeta = np.exp(log_gamma[:, None] * (f32(c - 1) - pos)) * f32(RET_DK ** -0.5)
    xi_b = np.broadcast_to(xi[:, :, None], (RET_HEADS, c, RET_DK)).astype(f32)
    zeta_t = np.broadcast_to(zeta[:, None, :], (RET_HEADS, RET_DK, c)).astype(f32)
    chunk_decay = np.exp(log_gamma * f32(c)).astype(f32)
    return decay, xi_b, zeta_t, chunk_decay


def _alibi_bias():
    f32 = np.float32
    t = CHUNK
    i = np.arange(t)[:, None]
    j = np.arange(t)[None, :]
    dist = np.where(j <= i, i - j, i + t - j).astype(f32)
    slopes = np.exp2(-8.0 * (np.arange(ATT_HEADS, dtype=f32) + f32(1)) / f32(ATT_HEADS)).astype(f32)
    bias = (f32(LOG2E) * slopes[:, None, None] * dist[None]).astype(f32)
    first = np.where((j <= i)[None], bias, f32(BIG)).astype(f32)
    return np.ascontiguousarray(np.swapaxes(np.stack([bias, first]), -1, -2))


def kernel(x, g_pre, w_in, sinks, w_out, g_post):
    bsz, seq, d_model = x.shape
    assert d_model == D_MODEL and w_in.shape == (D_MODEL, IN_COLS)
    assert w_out.shape == (D_MIX, D_MODEL) and D_MIX == D_MODEL and seq % TM == 0
    f32, bf16 = jnp.float32, jnp.bfloat16
    decay, xi_b, zeta_t, chunk_decay = _retention_tables()
    bias = _alibi_bias()
    n_tiles = bsz * seq // TM
    x2 = x.reshape(bsz * seq, D_MODEL)

    def const(shape):
        return pl.BlockSpec(shape, lambda n: (0,) * len(shape))

    smem = pl.BlockSpec(memory_space=pltpu.SMEM)
    out = pl.pallas_call(
        functools.partial(_layer_kernel, tm=TM, tiles_per_seq=seq // TM, n_tiles=n_tiles),
        grid=(n_tiles + 1,),
        in_specs=[
            smem,
            smem,
            pl.BlockSpec((TM, D_MODEL), lambda n: (jnp.minimum(n + 1, n_tiles - 1), 0)),
            pl.BlockSpec((TM, D_MODEL), lambda n: (jnp.maximum(n - 1, 0), 0)),
            const((1, D_MODEL)),
            pl.BlockSpec(memory_space=pl.ANY),
            pl.BlockSpec(memory_space=pl.ANY),
            const((1, D_MODEL)),
            const((RET_HEADS, CHUNK, CHUNK)),
            const((RET_HEADS, CHUNK, RET_DK)),
            const((RET_HEADS, RET_DK, CHUNK)),
            const((2, ATT_HEADS, CHUNK, CHUNK)),
        ],
        out_specs=pl.BlockSpec((TM, D_MODEL), lambda n: (jnp.maximum(n - 1, 0), 0)),
        out_shape=jax.ShapeDtypeStruct(x2.shape, x.dtype),
        scratch_shapes=[
            pltpu.VMEM((TM, IN_COLS), f32),
            pltpu.VMEM((TM, IN_COLS), f32),
            pltpu.VMEM((TM, D_MODEL), bf16),
            pltpu.VMEM((TM, D_MODEL), bf16),
            pltpu.VMEM((RET_HEADS, RET_DK, RET_DV), f32),
            pltpu.VMEM((N_CARRY, CHUNK, CHUNK), bf16),
            pltpu.VMEM((TM, D_MIX), bf16),
            pltpu.VMEM((D_MODEL, IN_COLS), bf16),
            pltpu.VMEM((D_MIX, D_MODEL), bf16),
            pltpu.VMEM((N_STAGE, D_MODEL, STAGE_COLS), f32),
            pltpu.SemaphoreType.DMA((N_STAGE,)),
        ],
        compiler_params=pltpu.CompilerParams(
            dimension_semantics=("arbitrary",),
            vmem_limit_bytes=VMEM_LIMIT_BYTES),
        name="hybrid_layer",
    )(sinks.astype(f32), chunk_decay, x2, x2, g_pre.reshape(1, D_MODEL).astype(f32),
      w_in.astype(f32), w_out.astype(f32), g_post.reshape(1, D_MODEL).astype(f32),
      decay, xi_b, zeta_t, bias)
    return out.reshape(x.shape)
```

```python
import functools

import jax
import jax.numpy as jnp
import numpy as np
from jax import lax
from jax.experimental import pallas as pl
from jax.experimental.pallas import tpu as pltpu

D_MODEL = 1024
RET_HEADS = 4
RET_DK = 128
RET_DV = 128
RET_WIDTH = RET_HEADS * RET_DV
CHUNK = 128
ATT_HEADS = 8
ATT_KV_HEADS = 2
ATT_GROUP = ATT_HEADS // ATT_KV_HEADS
ATT_HEAD_DIM = 64
ATT_WIDTH = ATT_HEADS * ATT_HEAD_DIM
ATT_KV_WIDTH = ATT_KV_HEADS * ATT_HEAD_DIM
D_MIX = RET_WIDTH + ATT_WIDTH
EPS = 1e-6
BIG = 1e30
LOG2E = 1.4426950408889634

COL_RQ = 0
COL_RK = COL_RQ + RET_HEADS * RET_DK
COL_RV = COL_RK + RET_HEADS * RET_DK
COL_RG = COL_RV + RET_WIDTH
COL_AQ = COL_RG + RET_WIDTH
COL_AK = COL_AQ + ATT_WIDTH
COL_AV = COL_AK + ATT_KV_WIDTH
COL_AG = COL_AV + ATT_KV_WIDTH
IN_COLS = COL_AG + ATT_WIDTH

PAIR = 2 * ATT_HEAD_DIM
assert PAIR == CHUNK and ATT_KV_WIDTH == CHUNK and RET_DK == CHUNK and RET_DV == CHUNK

TM = 512
VMEM_LIMIT_BYTES = 56 * 1024 * 1024
N_CARRY = 2
PROJ_COLS = 256
STAGE_COLS = 512
N_STAGE = 4
OUT_ROW_PARTS = 2
PIECES_AFTER = {"start": 1, "softmax": 1, "gate": 1, "out": 2}
NORMALISE_AFTER_PHASE = 4
assert IN_COLS % PROJ_COLS == 0


def _silu(x):
    hx = 0.5 * x
    return hx + hx * jnp.tanh(hx)


_dot = functools.partial(jnp.dot, preferred_element_type=jnp.float32)


def _retention_pair_scores(q_pair, k_pair):
    bf16 = jnp.bfloat16
    k_ts = (k_pair[:, :RET_DK].T, k_pair[:, RET_DK:].T)
    zeros = jnp.zeros((RET_DK, CHUNK), bf16)
    rhs = jnp.concatenate([jnp.concatenate([k_ts[0].astype(bf16), zeros], axis=1),
                           jnp.concatenate([zeros, k_ts[1].astype(bf16)], axis=1)], axis=0)
    return _dot(q_pair.astype(bf16), rhs), k_ts


def _retention_mix(sc, q, v, st, decay, xi):
    bf16 = jnp.bfloat16
    lhs = jnp.concatenate([(sc * decay).astype(bf16), (q * xi).astype(bf16)], axis=1)
    rhs = jnp.concatenate([v.astype(bf16), st.astype(bf16)], axis=0)
    return _dot(lhs, rhs)


def _retention_pair_kv(k_ts, v_pair, zeta_a, zeta_b):
    bf16 = jnp.bfloat16
    lhs = jnp.concatenate([(k_ts[0] * zeta_a).astype(bf16), (k_ts[1] * zeta_b).astype(bf16)], axis=1)
    vb = v_pair.astype(bf16)
    zeros = jnp.zeros((CHUNK, RET_DV), bf16)
    rhs = jnp.concatenate([jnp.concatenate([vb[:, :RET_DV], zeros], axis=1),
                           jnp.concatenate([zeros, vb[:, RET_DV:]], axis=1)], axis=0)
    return _dot(lhs, rhs)


def _retention_norm_gate(o, gate):
    mu = jnp.mean(o, axis=-1, keepdims=True)
    d = o - mu
    var = jnp.mean(d * d, axis=-1, keepdims=True)
    return d * lax.rsqrt(var + EPS) * _silu(gate)


def _attention_prep(k_both, v_both):
    bf16 = jnp.bfloat16
    return (k_both * (ATT_HEAD_DIM ** -0.5 * LOG2E)).astype(bf16), v_both.T.astype(bf16)


def _attention_scores(g, q_g, cur, prev):
    bf16 = jnp.bfloat16
    k_win = jnp.concatenate([cur[0], prev[0]], axis=0)
    zeros = jnp.zeros((ATT_HEAD_DIM, CHUNK), bf16)
    cols = []
    for p in range(2):
        q_t = q_g[:, p * PAIR:(p + 1) * PAIR].T.astype(bf16)
        for e in range(2):
            q_h = q_t[e * ATT_HEAD_DIM:(e + 1) * ATT_HEAD_DIM]
            cols.append(jnp.concatenate([q_h, zeros] if g == 0 else [zeros, q_h], axis=0))
    return _dot(k_win, jnp.concatenate(cols, axis=1))


def _attention_probs(g, s_t, bias_t_ref, first_idx, sinks_ref, tri_t):
    bf16 = jnp.bfloat16
    parts, inv_dens = [], []
    for hh in range(ATT_GROUP):
        hq = g * ATT_GROUP + hh
        s = s_t[:, hh * CHUNK:(hh + 1) * CHUNK]
        sf = jnp.where(tri_t, s[:CHUNK], s[CHUNK:]) - bias_t_ref[first_idx, hq]
        m = jnp.max(sf, axis=0, keepdims=True)
        ex = jnp.exp2(sf - m)
        den = jnp.sum(ex, axis=0, keepdims=True) + jnp.exp2(sinks_ref[hq] * LOG2E - m)
        inv_dens.append(1.0 / den)
        parts.append(jnp.concatenate([jnp.where(tri_t, ex, 0.0).astype(bf16),
                                      jnp.where(tri_t, 0.0, ex).astype(bf16)], axis=0))
    return jnp.concatenate(parts, axis=1), inv_dens


def _attention_pv(g, probs_t, cur, prev):
    out = _dot(jnp.concatenate([cur[1], prev[1]], axis=1), probs_t)
    return out[g * ATT_HEAD_DIM:(g + 1) * ATT_HEAD_DIM]


def _layer_kernel(sinks_ref, cdecay_ref, xnext_ref, xp_ref, gpre_ref, win_hbm, wout_hbm,
                  gpost_ref, decay_ref, xi_ref, zeta_t_ref, bias_ref,
                  o_ref, proj_a, proj_b, h_a, h_b, state_ref, carry_ref, mixed_ref,
                  win_ref, wout_ref, stage_ref, stage_sems, *, tm, tiles_per_seq, n_tiles):
    f32, bf16 = jnp.float32, jnp.bfloat16
    n = pl.program_id(0)
    first, last, even = n == 0, n == n_tiles, n % 2 == 0
    seq_start = (jnp.maximum(n - 1, 0) % tiles_per_seq) == 0

    def normalise(x_ref, h_ref):
        x = x_ref[...]
        ms = jnp.mean(x * x, axis=-1, keepdims=True)
        h_ref[...] = (x * lax.rsqrt(ms + EPS) * gpre_ref[...]).astype(bf16)

    @pl.when(seq_start)
    def _():
        state_ref[...] = jnp.zeros_like(state_ref)
        carry_ref[...] = jnp.zeros_like(carry_ref)

    row = lax.broadcasted_iota(jnp.int32, (CHUNK, CHUNK), 0)
    lane = lax.broadcasted_iota(jnp.int32, (CHUNK, CHUNK), 1)
    tri_t = row <= lane

    def project(h_ref, proj_w, row_parts=1):
        rp = tm // row_parts
        for j in range(0, IN_COLS, PROJ_COLS):
            for r in range(0, tm, rp):
                proj_w[r:r + rp, j:j + PROJ_COLS] = _dot(h_ref[r:r + rp, :], win_ref[:, j:j + PROJ_COLS])
            yield

    def mix(proj_r):
        prev = tuple(carry_ref[i] for i in range(N_CARRY))
        pairs, groups = range(RET_HEADS // 2), range(ATT_KV_HEADS)
        for c in range(tm // CHUNK):
            rows = slice(c * CHUNK, (c + 1) * CHUNK)

            def cols(base, i, width=CHUNK):
                return proj_r[rows, base + i * width:base + (i + 1) * width]

            scored = [_retention_pair_scores(cols(COL_RQ, hp, 2 * RET_DK), cols(COL_RK, hp, 2 * RET_DK))
                      for hp in pairs]
            cur = _attention_prep(cols(COL_AK, 0), cols(COL_AV, 0))
            s_t = [_attention_scores(g, cols(COL_AQ, g, 2 * PAIR), cur, prev) for g in groups]
            yield "scores"
            ret = []
            for hd in range(RET_HEADS):
                sc = scored[hd // 2][0][:, (hd % 2) * CHUNK:(hd % 2 + 1) * CHUNK]
                ret.append(_retention_mix(sc, cols(COL_RQ, hd), cols(COL_RV, hd), state_ref[hd],
                                          decay_ref[hd], xi_ref[hd]))
            for hp in pairs:
                kv = _retention_pair_kv(scored[hp][1], cols(COL_RV, hp, 2 * RET_DV),
                                        zeta_t_ref[2 * hp], zeta_t_ref[2 * hp + 1])
                for e in range(2):
                    hd = 2 * hp + e
                    state_ref[hd] = cdecay_ref[hd] * state_ref[hd] + kv[:, e * RET_DV:(e + 1) * RET_DV]
            yield "mix"
            for hd in range(RET_HEADS):
                mixed_ref[rows, hd * RET_DV:(hd + 1) * RET_DV] = _retention_norm_gate(
                    ret[hd], cols(COL_RG, hd)).astype(bf16)
            first_idx = seq_start.astype(jnp.int32) if c == 0 else 0
            probs = [_attention_probs(g, s_t[g], bias_ref, first_idx, sinks_ref, tri_t)
                     for g in groups]
            yield "softmax"
            pv = [_attention_pv(g, probs[g][0], cur, prev) for g in groups]
            yield "pv"
            for g in groups:
                for p in range(2):
                    i = g * 2 + p
                    heads = (2 * p, 2 * p + 1)
                    pair_t = jnp.concatenate(
                        [pv[g][:, hh * CHUNK:(hh + 1) * CHUNK] * probs[g][1][hh] for hh in heads], axis=0)
                    att = pair_t.T
                    mixed_ref[rows, RET_WIDTH + i * PAIR:RET_WIDTH + (i + 1) * PAIR] = (
                        att * _silu(cols(COL_AG, i))).astype(bf16)
            prev = cur
            yield "gate"
        for i in range(N_CARRY):
            carry_ref[i] = prev[i]

        rp = tm // OUT_ROW_PARTS
        for r in range(0, tm, rp):
            out = _dot(mixed_ref[r:r + rp, :], wout_ref[...])
            ms2 = jnp.mean(out * out, axis=-1, keepdims=True)
            o_ref[r:r + rp, :] = xp_ref[r:r + rp, :] + out * lax.rsqrt(ms2 + EPS) * gpost_ref[...]
            yield "out"

    def step(h_cur, h_next, proj_w, proj_r):
        pieces = project(h_cur, proj_w)
        for _ in range(PIECES_AFTER["start"]):
            next(pieces, None)
        for i, phase in enumerate(mix(proj_r)):
            for _ in range(PIECES_AFTER.get(phase, 0)):
                next(pieces, None)
            if i == NORMALISE_AFTER_PHASE:
                normalise(xnext_ref, h_next)
        for _ in pieces:
            pass

    w_blocks = ([(win_hbm, win_ref, j, min(STAGE_COLS, IN_COLS - j)) for j in range(0, IN_COLS, STAGE_COLS)]
                + [(wout_hbm, wout_ref, j, STAGE_COLS) for j in range(0, D_MODEL, STAGE_COLS)])

    def weight_copy(k):
        src, _, j, w = w_blocks[k]
        slot = k % N_STAGE
        return pltpu.make_async_copy(src.at[:, pl.ds(j, w)], stage_ref.at[slot, :, pl.ds(0, w)],
                                     stage_sems.at[slot])

    n_win_blocks = sum(1 for blk in w_blocks if blk[1] is win_ref)
    assert len(w_blocks) - n_win_blocks <= N_STAGE

    def land(k):
        _, dst, j, w = w_blocks[k]
        weight_copy(k).wait()
        dst[:, j:j + w] = stage_ref[k % N_STAGE, :, 0:w].astype(bf16)

    def first_step():
        for k in range(N_STAGE):
            weight_copy(k).start()
        normalise(xp_ref, h_a)
        pieces = project(h_a, proj_a, row_parts=2)
        n_late = len(w_blocks) - n_win_blocks
        for k in range(n_win_blocks):
            land(k)
            if k + N_STAGE < n_win_blocks:
                weight_copy(k + N_STAGE).start()
            if k >= n_win_blocks - n_late:
                weight_copy(k + n_late).start()
            if k >= 1:
                for _ in range(w_blocks[k - 1][3] // PROJ_COLS):
                    next(pieces, None)
        for _ in pieces:
            pass
        normalise(xnext_ref, h_b)

    @pl.when(n == 1)
    def _():
        for k in range(n_win_blocks, len(w_blocks)):
            land(k)

    pl.when(first)(first_step)

    @pl.when(last)
    def _():
        for _ in mix(proj_b if n_tiles % 2 == 0 else proj_a):
            pass

    @pl.when(even & jnp.logical_not(first | last))
    def _():
        step(h_a, h_b, proj_a, proj_b)

    @pl.when(jnp.logical_not(even | last))
    def _():
        step(h_b, h_a, proj_b, proj_a)


def _retention_tables():
    f32 = np.float32
    c = CHUNK
    log_gamma = np.log1p(-np.exp2(-5.0 - np.arange(RET_HEADS, dtype=f32))).astype(f32)
    pos = np.arange(c, dtype=f32)
    diff = pos[:, None] - pos[None, :]
    intra = np.where(diff >= 0, np.exp(log_gamma[:, None, None] * np.maximum(diff, f32(0))), f32(0))
    decay = (intra * f32(RET_DK ** -0.5)).astype(f32)
    xi = np.exp(log_gamma[:, None] * (pos + f32(1)))
    zeta = np.exp(log_gamma[:, None] * (f32(c - 1) - pos)) * f32(RET_DK ** -0.5)
    xi_b = np.broadcast_to(xi[:, :, None], (RET_HEADS, c, RET_DK)).astype(f32)
    zeta_t = np.broadcast_to(zeta[:, None, :], (RET_HEADS, RET_DK, c)).astype(f32)
    chunk_decay = np.exp(log_gamma * f32(c)).astype(f32)
    return decay, xi_b, zeta_t, chunk_decay


def _alibi_bias():
    f32 = np.float32
    t = CHUNK
    i = np.arange(t)[:, None]
    j = np.arange(t)[None, :]
    dist = np.where(j <= i, i - j, i + t - j).astype(f32)
    slopes = np.exp2(-8.0 * (np.arange(ATT_HEADS, dtype=f32) + f32(1)) / f32(ATT_HEADS)).astype(f32)
    bias = (f32(LOG2E) * slopes[:, None, None] * dist[None]).astype(f32)
    first = np.where((j <= i)[None], bias, f32(BIG)).astype(f32)
    return np.ascontiguousarray(np.swapaxes(np.stack([bias, first]), -1, -2))


def kernel(x, g_pre, w_in, sinks, w_out, g_post):
    bsz, seq, d_model = x.shape
    assert d_model == D_MODEL and w_in.shape == (D_MODEL, IN_COLS)
    assert w_out.shape == (D_MIX, D_MODEL) and D_MIX == D_MODEL and seq % TM == 0
    f32, bf16 = jnp.float32, jnp.bfloat16
    decay, xi_b, zeta_t, chunk_decay = _retention_tables()
    bias = _alibi_bias()
    n_tiles = bsz * seq // TM
    x2 = x.reshape(bsz * seq, D_MODEL)

    def const(shape):
        return pl.BlockSpec(shape, lambda n: (0,) * len(shape))

    smem = pl.BlockSpec(memory_space=pltpu.SMEM)
    out = pl.pallas_call(
        functools.partial(_layer_kernel, tm=TM, tiles_per_seq=seq // TM, n_tiles=n_tiles),
        grid=(n_tiles + 1,),
        in_specs=[
            smem,
            smem,
            pl.BlockSpec((TM, D_MODEL), lambda n: (jnp.minimum(n + 1, n_tiles - 1), 0)),
            pl.BlockSpec((TM, D_MODEL), lambda n: (jnp.maximum(n - 1, 0), 0)),
            const((1, D_MODEL)),
            pl.BlockSpec(memory_space=pl.ANY),
            pl.BlockSpec(memory_space=pl.ANY),
            const((1, D_MODEL)),
            const((RET_HEADS, CHUNK, CHUNK)),
            const((RET_HEADS, CHUNK, RET_DK)),
            const((RET_HEADS, RET_DK, CHUNK)),
            const((2, ATT_HEADS, CHUNK, CHUNK)),
        ],
        out_specs=pl.BlockSpec((TM, D_MODEL), lambda n: (jnp.maximum(n - 1, 0), 0)),
        out_shape=jax.ShapeDtypeStruct(x2.shape, x.dtype),
        scratch_shapes=[
            pltpu.VMEM((TM, IN_COLS), f32),
            pltpu.VMEM((TM, IN_COLS), f32),
            pltpu.VMEM((TM, D_MODEL), bf16),
            pltpu.VMEM((TM, D_MODEL), bf16),
            pltpu.VMEM((RET_HEADS, RET_DK, RET_DV), f32),
            pltpu.VMEM((N_CARRY, CHUNK, CHUNK), bf16),
            pltpu.VMEM((TM, D_MIX), bf16),
            pltpu.VMEM((D_MODEL, IN_COLS), bf16),
            pltpu.VMEM((D_MIX, D_MODEL), bf16),
            pltpu.VMEM((N_STAGE, D_MODEL, STAGE_COLS), f32),
            pltpu.SemaphoreType.DMA((N_STAGE,)),
        ],
        compiler_params=pltpu.CompilerParams(
            dimension_semantics=("arbitrary",),
            vmem_limit_bytes=VMEM_LIMIT_BYTES),
        name="hybrid_layer",
    )(sinks.astype(f32), chunk_decay, x2, x2, g_pre.reshape(1, D_MODEL).astype(f32),
      w_in.astype(f32), w_out.astype(f32), g_post.reshape(1, D_MODEL).astype(f32),
      decay, xi_b, zeta_t, bias)
    return out.reshape(x.shape)
```

```python
import functools

import jax
import jax.numpy as jnp
import numpy as np
from jax import lax
from jax.experimental import pallas as pl
from jax.experimental.pallas import tpu as pltpu

D_MODEL = 1024
RET_HEADS = 4
RET_DK = 128
RET_DV = 128
RET_WIDTH = RET_HEADS * RET_DV
CHUNK = 128
ATT_HEADS = 8
ATT_KV_HEADS = 2
ATT_GROUP = ATT_HEADS // ATT_KV_HEADS
ATT_HEAD_DIM = 64
ATT_WIDTH = ATT_HEADS * ATT_HEAD_DIM
ATT_KV_WIDTH = ATT_KV_HEADS * ATT_HEAD_DIM
D_MIX = RET_WIDTH + ATT_WIDTH
EPS = 1e-6
BIG = 1e30
LOG2E = 1.4426950408889634

COL_RQ = 0
COL_RK = COL_RQ + RET_HEADS * RET_DK
COL_RV = COL_RK + RET_HEADS * RET_DK
COL_RG = COL_RV + RET_WIDTH
COL_AQ = COL_RG + RET_WIDTH
COL_AK = COL_AQ + ATT_WIDTH
COL_AV = COL_AK + ATT_KV_WIDTH
COL_AG = COL_AV + ATT_KV_WIDTH
IN_COLS = COL_AG + ATT_WIDTH

PAIR = 2 * ATT_HEAD_DIM
assert PAIR == CHUNK and ATT_KV_WIDTH == CHUNK and RET_DK == CHUNK and RET_DV == CHUNK

TM = 512
VMEM_LIMIT_BYTES = 56 * 1024 * 1024
N_CARRY = 2
PROJ_COLS = 256
STAGE_COLS = 512
N_STAGE = 4
OUT_ROW_PARTS = 2
PIECES_AFTER = {"start": 1, "softmax": 1, "out": 2}
NORMALISE_AFTER_PHASE = 4
assert IN_COLS % PROJ_COLS == 0


def _silu(x):
    hx = 0.5 * x
    return hx + hx * jnp.tanh(hx)


_dot = functools.partial(jnp.dot, preferred_element_type=jnp.float32)


def _retention_pair_scores(q_pair, k_pair):
    bf16 = jnp.bfloat16
    k_ts = (k_pair[:, :RET_DK].T, k_pair[:, RET_DK:].T)
    zeros = jnp.zeros((RET_DK, CHUNK), bf16)
    rhs = jnp.concatenate([jnp.concatenate([k_ts[0].astype(bf16), zeros], axis=1),
                           jnp.concatenate([zeros, k_ts[1].astype(bf16)], axis=1)], axis=0)
    return _dot(q_pair.astype(bf16), rhs), k_ts


def _retention_mix(sc, q, v, st, decay, xi):
    bf16 = jnp.bfloat16
    lhs = jnp.concatenate([(sc * decay).astype(bf16), (q * xi).astype(bf16)], axis=1)
    rhs = jnp.concatenate([v.astype(bf16), st.astype(bf16)], axis=0)
    return _dot(lhs, rhs)


def _retention_pair_kv(k_ts, v_pair, zeta_a, zeta_b):
    bf16 = jnp.bfloat16
    lhs = jnp.concatenate([(k_ts[0] * zeta_a).astype(bf16), (k_ts[1] * zeta_b).astype(bf16)], axis=1)
    vb = v_pair.astype(bf16)
    zeros = jnp.zeros((CHUNK, RET_DV), bf16)
    rhs = jnp.concatenate([jnp.concatenate([vb[:, :RET_DV], zeros], axis=1),
                           jnp.concatenate([zeros, vb[:, RET_DV:]], axis=1)], axis=0)
    return _dot(lhs, rhs)


def _retention_norm_gate(o, gate):
    mu = jnp.mean(o, axis=-1, keepdims=True)
    d = o - mu
    var = jnp.mean(d * d, axis=-1, keepdims=True)
    return d * lax.rsqrt(var + EPS) * _silu(gate)


def _attention_prep(k_both, v_both):
    bf16 = jnp.bfloat16
    return (k_both * (ATT_HEAD_DIM ** -0.5 * LOG2E)).astype(bf16), v_both.T.astype(bf16)


def _attention_scores(g, q_g, cur, prev):
    bf16 = jnp.bfloat16
    k_win = jnp.concatenate([cur[0], prev[0]], axis=0)
    zeros = jnp.zeros((ATT_HEAD_DIM, CHUNK), bf16)
    cols = []
    for p in range(2):
        q_t = q_g[:, p * PAIR:(p + 1) * PAIR].T.astype(bf16)
        for e in range(2):
            q_h = q_t[e * ATT_HEAD_DIM:(e + 1) * ATT_HEAD_DIM]
            cols.append(jnp.concatenate([q_h, zeros] if g == 0 else [zeros, q_h], axis=0))
    return _dot(k_win, jnp.concatenate(cols, axis=1))


def _attention_probs(g, s_t, bias_t_ref, first_idx, sinks_ref, tri_t):
    bf16 = jnp.bfloat16
    parts, inv_dens = [], []
    for hh in range(ATT_GROUP):
        hq = g * ATT_GROUP + hh
        s = s_t[:, hh * CHUNK:(hh + 1) * CHUNK]
        sf = jnp.where(tri_t, s[:CHUNK], s[CHUNK:]) - bias_t_ref[first_idx, hq]
        m = jnp.max(sf, axis=0, keepdims=True)
        ex = jnp.exp2(sf - m)
        den = jnp.sum(ex, axis=0, keepdims=True) + jnp.exp2(sinks_ref[hq] * LOG2E - m)
        inv_dens.append(1.0 / den)
        parts.append(jnp.concatenate([jnp.where(tri_t, ex, 0.0).astype(bf16),
                                      jnp.where(tri_t, 0.0, ex).astype(bf16)], axis=0))
    return jnp.concatenate(parts, axis=1), inv_dens


def _attention_pv(g, probs_t, cur, prev):
    out = _dot(jnp.concatenate([cur[1], prev[1]], axis=1), probs_t)
    return out[g * ATT_HEAD_DIM:(g + 1) * ATT_HEAD_DIM]


def _layer_kernel(sinks_ref, cdecay_ref, xnext_ref, xp_ref, gpre_ref, win_hbm, wout_hbm,
                  gpost_ref, decay_ref, xi_ref, zeta_t_ref, bias_ref,
                  o_ref, proj_a, proj_b, h_a, h_b, state_ref, carry_ref, mixed_ref,
                  win_ref, wout_ref, stage_ref, stage_sems, *, tm, tiles_per_seq, n_tiles):
    f32, bf16 = jnp.float32, jnp.bfloat16
    n = pl.program_id(0)
    first, last, even = n == 0, n == n_tiles, n % 2 == 0
    seq_start = (jnp.maximum(n - 1, 0) % tiles_per_seq) == 0

    def normalise(x_ref, h_ref):
        x = x_ref[...]
        ms = jnp.mean(x * x, axis=-1, keepdims=True)
        h_ref[...] = (x * lax.rsqrt(ms + EPS) * gpre_ref[...]).astype(bf16)

    @pl.when(seq_start)
    def _():
        state_ref[...] = jnp.zeros_like(state_ref)
        carry_ref[...] = jnp.zeros_like(carry_ref)

    row = lax.broadcasted_iota(jnp.int32, (CHUNK, CHUNK), 0)
    lane = lax.broadcasted_iota(jnp.int32, (CHUNK, CHUNK), 1)
    tri_t = row <= lane

    def project(h_ref, proj_w, row_parts=1):
        rp = tm // row_parts
        for j in range(0, IN_COLS, PROJ_COLS):
            for r in range(0, tm, rp):
                proj_w[r:r + rp, j:j + PROJ_COLS] = _dot(h_ref[r:r + rp, :], win_ref[:, j:j + PROJ_COLS])
            yield

    def mix(proj_r):
        prev = tuple(carry_ref[i] for i in range(N_CARRY))
        pairs, groups = range(RET_HEADS // 2), range(ATT_KV_HEADS)
        for c in range(tm // CHUNK):
            rows = slice(c * CHUNK, (c + 1) * CHUNK)

            def cols(base, i, width=CHUNK):
                return proj_r[rows, base + i * width:base + (i + 1) * width]

            scored = [_retention_pair_scores(cols(COL_RQ, hp, 2 * RET_DK), cols(COL_RK, hp, 2 * RET_DK))
                      for hp in pairs]
            cur = _attention_prep(cols(COL_AK, 0), cols(COL_AV, 0))
            yield "scores"
            ret = []
            for hd in range(RET_HEADS):
                sc = scored[hd // 2][0][:, (hd % 2) * CHUNK:(hd % 2 + 1) * CHUNK]
                ret.append(_retention_mix(sc, cols(COL_RQ, hd), cols(COL_RV, hd), state_ref[hd],
                                          decay_ref[hd], xi_ref[hd]))
            for hp in pairs:
                kv = _retention_pair_kv(scored[hp][1], cols(COL_RV, hp, 2 * RET_DV),
                                        zeta_t_ref[2 * hp], zeta_t_ref[2 * hp + 1])
                for e in range(2):
                    hd = 2 * hp + e
                    state_ref[hd] = cdecay_ref[hd] * state_ref[hd] + kv[:, e * RET_DV:(e + 1) * RET_DV]
            yield "mix"
            for hd in range(RET_HEADS):
                mixed_ref[rows, hd * RET_DV:(hd + 1) * RET_DV] = _retention_norm_gate(
                    ret[hd], cols(COL_RG, hd)).astype(bf16)
            first_idx = seq_start.astype(jnp.int32) if c == 0 else 0
            for g in groups:
                s_t = _attention_scores(g, cols(COL_AQ, g, 2 * PAIR), cur, prev)
                probs_t, inv_dens = _attention_probs(g, s_t, bias_ref, first_idx, sinks_ref, tri_t)
                yield "softmax"
                pv = _attention_pv(g, probs_t, cur, prev)
                for p in range(2):
                    i = g * 2 + p
                    heads = (2 * p, 2 * p + 1)
                    pair_t = jnp.concatenate(
                        [pv[:, hh * CHUNK:(hh + 1) * CHUNK] * inv_dens[hh] for hh in heads], axis=0)
                    att = pair_t.T
                    mixed_ref[rows, RET_WIDTH + i * PAIR:RET_WIDTH + (i + 1) * PAIR] = (
                        att * _silu(cols(COL_AG, i))).astype(bf16)
                yield "gate"
            prev = cur
        for i in range(N_CARRY):
            carry_ref[i] = prev[i]

        rp = tm // OUT_ROW_PARTS
        for r in range(0, tm, rp):
            out = _dot(mixed_ref[r:r + rp, :], wout_ref[...])
            ms2 = jnp.mean(out * out, axis=-1, keepdims=True)
            o_ref[r:r + rp, :] = xp_ref[r:r + rp, :] + out * lax.rsqrt(ms2 + EPS) * gpost_ref[...]
            yield "out"

    def step(h_cur, h_next, proj_w, proj_r):
        pieces = project(h_cur, proj_w)
        for _ in range(PIECES_AFTER["start"]):
            next(pieces, None)
        for i, phase in enumerate(mix(proj_r)):
            for _ in range(PIECES_AFTER.get(phase, 0)):
                next(pieces, None)
            if i == NORMALISE_AFTER_PHASE:
                normalise(xnext_ref, h_next)
        for _ in pieces:
            pass

    w_blocks = ([(win_hbm, win_ref, j, min(STAGE_COLS, IN_COLS - j)) for j in range(0, IN_COLS, STAGE_COLS)]
                + [(wout_hbm, wout_ref, j, STAGE_COLS) for j in range(0, D_MODEL, STAGE_COLS)])

    def weight_copy(k):
        src, _, j, w = w_blocks[k]
        slot = k % N_STAGE
        return pltpu.make_async_copy(src.at[:, pl.ds(j, w)], stage_ref.at[slot, :, pl.ds(0, w)],
                                     stage_sems.at[slot])

    n_win_blocks = sum(1 for blk in w_blocks if blk[1] is win_ref)
    assert len(w_blocks) - n_win_blocks <= N_STAGE

    def land(k):
        _, dst, j, w = w_blocks[k]
        weight_copy(k).wait()
        dst[:, j:j + w] = stage_ref[k % N_STAGE, :, 0:w].astype(bf16)

    def first_step():
        for k in range(N_STAGE):
            weight_copy(k).start()
        normalise(xp_ref, h_a)
        pieces = project(h_a, proj_a, row_parts=2)
        for k in range(n_win_blocks):
            land(k)
            if k + N_STAGE < len(w_blocks):
                weight_copy(k + N_STAGE).start()
            if k >= 1:
                for _ in range(w_blocks[k - 1][3] // PROJ_COLS):
                    next(pieces, None)
        for _ in pieces:
            pass
        normalise(xnext_ref, h_b)

    @pl.when(n == 1)
    def _():
        for k in range(n_win_blocks, len(w_blocks)):
            land(k)

    pl.when(first)(first_step)

    @pl.when(last)
    def _():
        for _ in mix(proj_b if n_tiles % 2 == 0 else proj_a):
            pass

    @pl.when(even & jnp.logical_not(first | last))
    def _():
        step(h_a, h_b, proj_a, proj_b)

    @pl.when(jnp.logical_not(even | last))
    def _():
        step(h_b, h_a, proj_b, proj_a)


def _retention_tables():
    f32 = np.float32
    c = CHUNK
    log_gamma = np.log1p(-np.exp2(-5.0 - np.arange(RET_HEADS, dtype=f32))).astype(f32)
    pos = np.arange(c, dtype=f32)
    diff = pos[:, None] - pos[None, :]
    intra = np.where(diff >= 0, np.exp(log_gamma[:, None, None] * np.maximum(diff, f32(0))), f32(0))
    decay = (intra * f32(RET_DK ** -0.5)).astype(f32)
    xi = np.exp(log_gamma[:, None] * (pos + f32(1)))
    zeta = np.exp(log_gamma[:, None] * (f32(c - 1) - pos)) * f32(RET_DK ** -0.5)
    xi_b = np.broadcast_to(xi[:, :, None], (RET_HEADS, c, RET_DK)).astype(f32)
    zeta_t = np.broadcast_to(zeta[:, None, :], (RET_HEADS, RET_DK, c)).astype(f32)
    chunk_decay = np.exp(log_gamma * f32(c)).astype(f32)
    return decay, xi_b, zeta_t, chunk_decay


def _alibi_bias():
    f32 = np.float32
    t = CHUNK
    i = np.arange(t)[:, None]
    j = np.arange(t)[None, :]
    dist = np.where(j <= i, i - j, i + t - j).astype(f32)
    slopes = np.exp2(-8.0 * (np.arange(ATT_HEADS, dtype=f32) + f32(1)) / f32(ATT_HEADS)).astype(f32)
    bias = (f32(LOG2E) * slopes[:, None, None] * dist[None]).astype(f32)
    first = np.where((j <= i)[None], bias, f32(BIG)).astype(f32)
    return np.ascontiguousarray(np.swapaxes(np.stack([bias, first]), -1, -2))


def kernel(x, g_pre, w_in, sinks, w_out, g_post):
    bsz, seq, d_model = x.shape
    assert d_model == D_MODEL and w_in.shape == (D_MODEL, IN_COLS)
    assert w_out.shape == (D_MIX, D_MODEL) and D_MIX == D_MODEL and seq % TM == 0
    f32, bf16 = jnp.float32, jnp.bfloat16
    decay, xi_b, zeta_t, chunk_decay = _retention_tables()
    bias = _alibi_bias()
    n_tiles = bsz * seq // TM
    x2 = x.reshape(bsz * seq, D_MODEL)

    def const(shape):
        return pl.BlockSpec(shape, lambda n: (0,) * len(shape))

    smem = pl.BlockSpec(memory_space=pltpu.SMEM)
    out = pl.pallas_call(
        functools.partial(_layer_kernel, tm=TM, tiles_per_seq=seq // TM, n_tiles=n_tiles),
        grid=(n_tiles + 1,),
        in_specs=[
            smem,
            smem,
            pl.BlockSpec((TM, D_MODEL), lambda n: (jnp.minimum(n + 1, n_tiles - 1), 0)),
            pl.BlockSpec((TM, D_MODEL), lambda n: (jnp.maximum(n - 1, 0), 0)),
            const((1, D_MODEL)),
            pl.BlockSpec(memory_space=pl.ANY),
            pl.BlockSpec(memory_space=pl.ANY),
            const((1, D_MODEL)),
            const((RET_HEADS, CHUNK, CHUNK)),
            const((RET_HEADS, CHUNK, RET_DK)),
            const((RET_HEADS, RET_DK, CHUNK)),
            const((2, ATT_HEADS, CHUNK, CHUNK)),
        ],
        out_specs=pl.BlockSpec((TM, D_MODEL), lambda n: (jnp.maximum(n - 1, 0), 0)),
        out_shape=jax.ShapeDtypeStruct(x2.shape, x.dtype),
        scratch_shapes=[
            pltpu.VMEM((TM, IN_COLS), f32),
            pltpu.VMEM((TM, IN_COLS), f32),
            pltpu.VMEM((TM, D_MODEL), bf16),
            pltpu.VMEM((TM, D_MODEL), bf16),
            pltpu.VMEM((RET_HEADS, RET_DK, RET_DV), f32),
            pltpu.VMEM((N_CARRY, CHUNK, CHUNK), bf16),
            pltpu.VMEM((TM, D_MIX), bf16),
            pltpu.VMEM((D_MODEL, IN_COLS), bf16),
            pltpu.VMEM((D_MIX, D_MODEL), bf16),
            pltpu.VMEM((N_STAGE, D_MODEL, STAGE_COLS), f32),
            pltpu.SemaphoreType.DMA((N_STAGE,)),
        ],
        compiler_params=pltpu.CompilerParams(
            dimension_semantics=("arbitrary",),
            vmem_limit_bytes=VMEM_LIMIT_BYTES),
        name="hybrid_layer",
    )(sinks.astype(f32), chunk_decay, x2, x2, g_pre.reshape(1, D_MODEL).astype(f32),
      w_in.astype(f32), w_out.astype(f32), g_post.reshape(1, D_MODEL).astype(f32),
      decay, xi_b, zeta_t, bias)
    return out.reshape(x.shape)
```

```python
import functools

import jax
import jax.numpy as jnp
import numpy as np
from jax import lax
from jax.experimental import pallas as pl
from jax.experimental.pallas import tpu as pltpu

D_MODEL = 1024
RET_HEADS = 4
RET_DK = 128
RET_DV = 128
RET_WIDTH = RET_HEADS * RET_DV
CHUNK = 128
ATT_HEADS = 8
ATT_KV_HEADS = 2
ATT_GROUP = ATT_HEADS // ATT_KV_HEADS
ATT_HEAD_DIM = 64
ATT_WIDTH = ATT_HEADS * ATT_HEAD_DIM
ATT_KV_WIDTH = ATT_KV_HEADS * ATT_HEAD_DIM
D_MIX = RET_WIDTH + ATT_WIDTH
EPS = 1e-6
BIG = 1e30
LOG2E = 1.4426950408889634

COL_RQ = 0
COL_RK = COL_RQ + RET_HEADS * RET_DK
COL_RV = COL_RK + RET_HEADS * RET_DK
COL_RG = COL_RV + RET_WIDTH
COL_AQ = COL_RG + RET_WIDTH
COL_AK = COL_AQ + ATT_WIDTH
COL_AV = COL_AK + ATT_KV_WIDTH
COL_AG = COL_AV + ATT_KV_WIDTH
IN_COLS = COL_AG + ATT_WIDTH

PAIR = 2 * ATT_HEAD_DIM
assert PAIR == CHUNK and ATT_KV_WIDTH == CHUNK and RET_DK == CHUNK and RET_DV == CHUNK

TM = 512
VMEM_LIMIT_BYTES = 56 * 1024 * 1024
N_CARRY = 2
PROJ_COLS = 256
STAGE_COLS = 512
N_STAGE = 4
OUT_ROW_PARTS = 2
PIECES_AFTER = {"start": 1, "softmax": 1, "gate": 1, "out": 2}
CHUNK_SPAN = 2
NORMALISE_AFTER_PHASE = 4
assert IN_COLS % PROJ_COLS == 0


def _silu(x):
    hx = 0.5 * x
    return hx + hx * jnp.tanh(hx)


_dot = functools.partial(jnp.dot, preferred_element_type=jnp.float32)


def _retention_pair_scores(q_pair, k_pair):
    bf16 = jnp.bfloat16
    k_ts = (k_pair[:, :RET_DK].T, k_pair[:, RET_DK:].T)
    zeros = jnp.zeros((RET_DK, CHUNK), bf16)
    rhs = jnp.concatenate([jnp.concatenate([k_ts[0].astype(bf16), zeros], axis=1),
                           jnp.concatenate([zeros, k_ts[1].astype(bf16)], axis=1)], axis=0)
    return _dot(q_pair.astype(bf16), rhs), k_ts


def _retention_mix(sc, q, v, st, decay, xi):
    bf16 = jnp.bfloat16
    lhs = jnp.concatenate([(sc * decay).astype(bf16), (q * xi).astype(bf16)], axis=1)
    rhs = jnp.concatenate([v.astype(bf16), st.astype(bf16)], axis=0)
    return _dot(lhs, rhs)


def _retention_pair_kv(k_ts, v_pair, zeta_a, zeta_b):
    bf16 = jnp.bfloat16
    lhs = jnp.concatenate([(k_ts[0] * zeta_a).astype(bf16), (k_ts[1] * zeta_b).astype(bf16)], axis=1)
    vb = v_pair.astype(bf16)
    zeros = jnp.zeros((CHUNK, RET_DV), bf16)
    rhs = jnp.concatenate([jnp.concatenate([vb[:, :RET_DV], zeros], axis=1),
                           jnp.concatenate([zeros, vb[:, RET_DV:]], axis=1)], axis=0)
    return _dot(lhs, rhs)


def _retention_norm_gate(o, gate):
    mu = jnp.mean(o, axis=-1, keepdims=True)
    d = o - mu
    var = jnp.mean(d * d, axis=-1, keepdims=True)
    return d * lax.rsqrt(var + EPS) * _silu(gate)


def _attention_prep(k_both, v_both):
    bf16 = jnp.bfloat16
    return (k_both * (ATT_HEAD_DIM ** -0.5 * LOG2E)).astype(bf16), v_both.T.astype(bf16)


def _attention_scores(g, q_g, cur, prev):
    bf16 = jnp.bfloat16
    k_win = jnp.concatenate([cur[0], prev[0]], axis=0)
    zeros = jnp.zeros((ATT_HEAD_DIM, CHUNK), bf16)
    cols = []
    for p in range(2):
        q_t = q_g[:, p * PAIR:(p + 1) * PAIR].T.astype(bf16)
        for e in range(2):
            q_h = q_t[e * ATT_HEAD_DIM:(e + 1) * ATT_HEAD_DIM]
            cols.append(jnp.concatenate([q_h, zeros] if g == 0 else [zeros, q_h], axis=0))
    return _dot(k_win, jnp.concatenate(cols, axis=1))


def _attention_probs(g, s_t, bias_t_ref, first_idx, sinks_ref, tri_t):
    bf16 = jnp.bfloat16
    parts, inv_dens = [], []
    for hh in range(ATT_GROUP):
        hq = g * ATT_GROUP + hh
        s = s_t[:, hh * CHUNK:(hh + 1) * CHUNK]
        sf = jnp.where(tri_t, s[:CHUNK], s[CHUNK:]) - bias_t_ref[first_idx, hq]
        m = jnp.max(sf, axis=0, keepdims=True)
        ex = jnp.exp2(sf - m)
        den = jnp.sum(ex, axis=0, keepdims=True) + jnp.exp2(sinks_ref[hq] * LOG2E - m)
        inv_dens.append(1.0 / den)
        parts.append(jnp.concatenate([jnp.where(tri_t, ex, 0.0).astype(bf16),
                                      jnp.where(tri_t, 0.0, ex).astype(bf16)], axis=0))
    return jnp.concatenate(parts, axis=1), inv_dens


def _attention_pv(g, probs_t, cur, prev):
    out = _dot(jnp.concatenate([cur[1], prev[1]], axis=1), probs_t)
    return out[g * ATT_HEAD_DIM:(g + 1) * ATT_HEAD_DIM]


def _layer_kernel(sinks_ref, cdecay_ref, xnext_ref, xp_ref, gpre_ref, win_hbm, wout_hbm,
                  gpost_ref, decay_ref, xi_ref, zeta_t_ref, bias_ref,
                  o_ref, proj_a, proj_b, h_a, h_b, state_ref, carry_ref, mixed_ref,
                  win_ref, wout_ref, stage_ref, stage_sems, *, tm, tiles_per_seq, n_tiles):
    f32, bf16 = jnp.float32, jnp.bfloat16
    n = pl.program_id(0)
    first, last, even = n == 0, n == n_tiles, n % 2 == 0
    seq_start = (jnp.maximum(n - 1, 0) % tiles_per_seq) == 0

    def normalise(x_ref, h_ref):
        x = x_ref[...]
        ms = jnp.mean(x * x, axis=-1, keepdims=True)
        h_ref[...] = (x * lax.rsqrt(ms + EPS) * gpre_ref[...]).astype(bf16)

    @pl.when(seq_start)
    def _():
        state_ref[...] = jnp.zeros_like(state_ref)
        carry_ref[...] = jnp.zeros_like(carry_ref)

    row = lax.broadcasted_iota(jnp.int32, (CHUNK, CHUNK), 0)
    lane = lax.broadcasted_iota(jnp.int32, (CHUNK, CHUNK), 1)
    tri_t = row <= lane

    def project(h_ref, proj_w, row_parts=1):
        rp = tm // row_parts
        for j in range(0, IN_COLS, PROJ_COLS):
            for r in range(0, tm, rp):
                proj_w[r:r + rp, j:j + PROJ_COLS] = _dot(h_ref[r:r + rp, :], win_ref[:, j:j + PROJ_COLS])
            yield

    def mix(proj_r):
        carry = tuple(carry_ref[i] for i in range(N_CARRY))
        pairs, groups = range(RET_HEADS // 2), range(ATT_KV_HEADS)

        def cols_of(c):
            rows = slice(c * CHUNK, (c + 1) * CHUNK)
            return lambda base, i, width=CHUNK: proj_r[rows, base + i * width:base + (i + 1) * width]

        for c0 in range(0, tm // CHUNK, CHUNK_SPAN):
            span = range(c0, c0 + CHUNK_SPAN)
            col = {c: cols_of(c) for c in span}
            scored = {c: [_retention_pair_scores(col[c](COL_RQ, hp, 2 * RET_DK), col[c](COL_RK, hp, 2 * RET_DK))
                          for hp in pairs] for c in span}
            kvs = {c0 - 1: carry}
            s_t = {}
            for c in span:
                kvs[c] = _attention_prep(col[c](COL_AK, 0), col[c](COL_AV, 0))
                s_t[c] = [_attention_scores(g, col[c](COL_AQ, g, 2 * PAIR), kvs[c], kvs[c - 1]) for g in groups]
            yield "scores"
            ret = {}
            for c in span:
                ret[c] = []
                for hd in range(RET_HEADS):
                    sc = scored[c][hd // 2][0][:, (hd % 2) * CHUNK:(hd % 2 + 1) * CHUNK]
                    ret[c].append(_retention_mix(sc, col[c](COL_RQ, hd), col[c](COL_RV, hd), state_ref[hd],
                                                 decay_ref[hd], xi_ref[hd]))
                for hp in pairs:
                    kv = _retention_pair_kv(scored[c][hp][1], col[c](COL_RV, hp, 2 * RET_DV),
                                            zeta_t_ref[2 * hp], zeta_t_ref[2 * hp + 1])
                    for e in range(2):
                        hd = 2 * hp + e
                        state_ref[hd] = cdecay_ref[hd] * state_ref[hd] + kv[:, e * RET_DV:(e + 1) * RET_DV]
            yield "mix"
            probs = {}
            for c in span:
                rows = slice(c * CHUNK, (c + 1) * CHUNK)
                for hd in range(RET_HEADS):
                    mixed_ref[rows, hd * RET_DV:(hd + 1) * RET_DV] = _retention_norm_gate(
                        ret[c][hd], col[c](COL_RG, hd)).astype(bf16)
                first_idx = seq_start.astype(jnp.int32) if c == 0 else 0
                probs[c] = [_attention_probs(g, s_t[c][g], bias_ref, first_idx, sinks_ref, tri_t)
                            for g in groups]
                yield "softmax"
            pv = {c: [_attention_pv(g, probs[c][g][0], kvs[c], kvs[c - 1]) for g in groups] for c in span}
            yield "pv"
            for c in span:
                rows = slice(c * CHUNK, (c + 1) * CHUNK)
                for g in groups:
                    for p in range(2):
                        i = g * 2 + p
                        heads = (2 * p, 2 * p + 1)
                        pair_t = jnp.concatenate(
                            [pv[c][g][:, hh * CHUNK:(hh + 1) * CHUNK] * probs[c][g][1][hh] for hh in heads],
                            axis=0)
                        att = pair_t.T
                        mixed_ref[rows, RET_WIDTH + i * PAIR:RET_WIDTH + (i + 1) * PAIR] = (
                            att * _silu(col[c](COL_AG, i))).astype(bf16)
                yield "gate"
            carry = kvs[c0 + CHUNK_SPAN - 1]
        for i in range(N_CARRY):
            carry_ref[i] = carry[i]

        rp = tm // OUT_ROW_PARTS
        for r in range(0, tm, rp):
            out = _dot(mixed_ref[r:r + rp, :], wout_ref[...])
            ms2 = jnp.mean(out * out, axis=-1, keepdims=True)
            o_ref[r:r + rp, :] = xp_ref[r:r + rp, :] + out * lax.rsqrt(ms2 + EPS) * gpost_ref[...]
            yield "out"

    def step(h_cur, h_next, proj_w, proj_r):
        pieces = project(h_cur, proj_w)
        for _ in range(PIECES_AFTER["start"]):
            next(pieces, None)
        for i, phase in enumerate(mix(proj_r)):
            for _ in range(PIECES_AFTER.get(phase, 0)):
                next(pieces, None)
            if i == NORMALISE_AFTER_PHASE:
                normalise(xnext_ref, h_next)
        for _ in pieces:
            pass

    w_blocks = ([(win_hbm, win_ref, j, min(STAGE_COLS, IN_COLS - j)) for j in range(0, IN_COLS, STAGE_COLS)]
                + [(wout_hbm, wout_ref, j, STAGE_COLS) for j in range(0, D_MODEL, STAGE_COLS)])

    def weight_copy(k):
        src, _, j, w = w_blocks[k]
        slot = k % N_STAGE
        return pltpu.make_async_copy(src.at[:, pl.ds(j, w)], stage_ref.at[slot, :, pl.ds(0, w)],
                                     stage_sems.at[slot])

    n_win_blocks = sum(1 for blk in w_blocks if blk[1] is win_ref)
    assert len(w_blocks) - n_win_blocks <= N_STAGE

    def land(k):
        _, dst, j, w = w_blocks[k]
        weight_copy(k).wait()
        dst[:, j:j + w] = stage_ref[k % N_STAGE, :, 0:w].astype(bf16)

    def first_step():
        for k in range(N_STAGE):
            weight_copy(k).start()
        normalise(xp_ref, h_a)
        pieces = project(h_a, proj_a, row_parts=2)
        for k in range(n_win_blocks):
            land(k)
            if k + N_STAGE < len(w_blocks):
                weight_copy(k + N_STAGE).start()
            if k >= 1:
                for _ in range(w_blocks[k - 1][3] // PROJ_COLS):
                    next(pieces, None)
        for _ in pieces:
            pass
        normalise(xnext_ref, h_b)

    @pl.when(n == 1)
    def _():
        for k in range(n_win_blocks, len(w_blocks)):
            land(k)

    pl.when(first)(first_step)

    @pl.when(last)
    def _():
        for _ in mix(proj_b if n_tiles % 2 == 0 else proj_a):
            pass

    @pl.when(even & jnp.logical_not(first | last))
    def _():
        step(h_a, h_b, proj_a, proj_b)

    @pl.when(jnp.logical_not(even | last))
    def _():
        step(h_b, h_a, proj_b, proj_a)


def _retention_tables():
    f32 = np.float32
    c = CHUNK
    log_gamma = np.log1p(-np.exp2(-5.0 - np.arange(RET_HEADS, dtype=f32))).astype(f32)
    pos = np.arange(c, dtype=f32)
    diff = pos[:, None] - pos[None, :]
    intra = np.where(diff >= 0, np.exp(log_gamma[:, None, None] * np.maximum(diff, f32(0))), f32(0))
    decay = (intra * f32(RET_DK ** -0.5)).astype(f32)
    xi = np.exp(log_gamma[:, None] * (pos + f32(1)))
    zeta = np.exp(log_gamma[:, None] * (f32(c - 1) - pos)) * f32(RET_DK ** -0.5)
    xi_b = np.broadcast_to(xi[:, :, None], (RET_HEADS, c, RET_DK)).astype(f32)
    zeta_t = np.broadcast_to(zeta[:, None, :], (RET_HEADS, RET_DK, c)).astype(f32)
    chunk_decay = np.exp(log_gamma * f32(c)).astype(f32)
    return decay, xi_b, zeta_t, chunk_decay


def _alibi_bias():
    f32 = np.float32
    t = CHUNK
    i = np.arange(t)[:, None]
    j = np.arange(t)[None, :]
    dist = np.where(j <= i, i - j, i + t - j).astype(f32)
    slopes = np.exp2(-8.0 * (np.arange(ATT_HEADS, dtype=f32) + f32(1)) / f32(ATT_HEADS)).astype(f32)
    bias = (f32(LOG2E) * slopes[:, None, None] * dist[None]).astype(f32)
    first = np.where((j <= i)[None], bias, f32(BIG)).astype(f32)
    return np.ascontiguousarray(np.swapaxes(np.stack([bias, first]), -1, -2))


def kernel(x, g_pre, w_in, sinks, w_out, g_post):
    bsz, seq, d_model = x.shape
    assert d_model == D_MODEL and w_in.shape == (D_MODEL, IN_COLS)
    assert w_out.shape == (D_MIX, D_MODEL) and D_MIX == D_MODEL and seq % TM == 0
    f32, bf16 = jnp.float32, jnp.bfloat16
    decay, xi_b, zeta_t, chunk_decay = _retention_tables()
    bias = _alibi_bias()
    n_tiles = bsz * seq // TM
    x2 = x.reshape(bsz * seq, D_MODEL)

    def const(shape):
        return pl.BlockSpec(shape, lambda n: (0,) * len(shape))

    smem = pl.BlockSpec(memory_space=pltpu.SMEM)
    out = pl.pallas_call(
        functools.partial(_layer_kernel, tm=TM, tiles_per_seq=seq // TM, n_tiles=n_tiles),
        grid=(n_tiles + 1,),
        in_specs=[
            smem,
            smem,
            pl.BlockSpec((TM, D_MODEL), lambda n: (jnp.minimum(n + 1, n_tiles - 1), 0)),
            pl.BlockSpec((TM, D_MODEL), lambda n: (jnp.maximum(n - 1, 0), 0)),
            const((1, D_MODEL)),
            pl.BlockSpec(memory_space=pl.ANY),
            pl.BlockSpec(memory_space=pl.ANY),
            const((1, D_MODEL)),
            const((RET_HEADS, CHUNK, CHUNK)),
            const((RET_HEADS, CHUNK, RET_DK)),
            const((RET_HEADS, RET_DK, CHUNK)),
            const((2, ATT_HEADS, CHUNK, CHUNK)),
        ],
        out_specs=pl.BlockSpec((TM, D_MODEL), lambda n: (jnp.maximum(n - 1, 0), 0)),
        out_shape=jax.ShapeDtypeStruct(x2.shape, x.dtype),
        scratch_shapes=[
            pltpu.VMEM((TM, IN_COLS), f32),
            pltpu.VMEM((TM, IN_COLS), f32),
            pltpu.VMEM((TM, D_MODEL), bf16),
            pltpu.VMEM((TM, D_MODEL), bf16),
            pltpu.VMEM((RET_HEADS, RET_DK, RET_DV), f32),
            pltpu.VMEM((N_CARRY, CHUNK, CHUNK), bf16),
            pltpu.VMEM((TM, D_MIX), bf16),
            pltpu.VMEM((D_MODEL, IN_COLS), bf16),
            pltpu.VMEM((D_MIX, D_MODEL), bf16),
            pltpu.VMEM((N_STAGE, D_MODEL, STAGE_COLS), f32),
            pltpu.SemaphoreType.DMA((N_STAGE,)),
        ],
        compiler_params=pltpu.CompilerParams(
            dimension_semantics=("arbitrary",),
            vmem_limit_bytes=VMEM_LIMIT_BYTES),
        name="hybrid_layer",
    )(sinks.astype(f32), chunk_decay, x2, x2, g_pre.reshape(1, D_MODEL).astype(f32),
      w_in.astype(f32), w_out.astype(f32), g_post.reshape(1, D_MODEL).astype(f32),
      decay, xi_b, zeta_t, bias)
    return out.reshape(x.shape)
```

```python
import functools

import jax
import jax.numpy as jnp
import numpy as np
from jax import lax
from jax.experimental import pallas as pl
from jax.experimental.pallas import tpu as pltpu

D_MODEL = 1024
RET_HEADS = 4
RET_DK = 128
RET_DV = 128
RET_WIDTH = RET_HEADS * RET_DV
CHUNK = 128
ATT_HEADS = 8
ATT_KV_HEADS = 2
ATT_GROUP = ATT_HEADS // ATT_KV_HEADS
ATT_HEAD_DIM = 64
ATT_WIDTH = ATT_HEADS * ATT_HEAD_DIM
ATT_KV_WIDTH = ATT_KV_HEADS * ATT_HEAD_DIM
D_MIX = RET_WIDTH + ATT_WIDTH
EPS = 1e-6
BIG = 1e30
LOG2E = 1.4426950408889634

COL_RQ = 0
COL_RK = COL_RQ + RET_HEADS * RET_DK
COL_RV = COL_RK + RET_HEADS * RET_DK
COL_RG = COL_RV + RET_WIDTH
COL_AQ = COL_RG + RET_WIDTH
COL_AK = COL_AQ + ATT_WIDTH
COL_AV = COL_AK + ATT_KV_WIDTH
COL_AG = COL_AV + ATT_KV_WIDTH
IN_COLS = COL_AG + ATT_WIDTH

PAIR = 2 * ATT_HEAD_DIM
assert PAIR == CHUNK and ATT_KV_WIDTH == CHUNK and RET_DK == CHUNK and RET_DV == CHUNK

TM = 512
VMEM_LIMIT_BYTES = 56 * 1024 * 1024
N_CARRY = 2
PROJ_COLS = 256
STAGE_COLS = 512
N_STAGE = 4
OUT_ROW_PARTS = 2
PIECES_AFTER = {"start": 1, "softmax": 1, "gate": 1, "out": 2}
NORMALISE_AFTER_PHASE = 4
assert IN_COLS % PROJ_COLS == 0


def _silu(x):
    hx = 0.5 * x
    return hx + hx * jnp.tanh(hx)


_dot = functools.partial(jnp.dot, preferred_element_type=jnp.float32)


def _retention_pair_scores(q_pair, k_pair):
    bf16 = jnp.bfloat16
    k_ts = (k_pair[:, :RET_DK].T, k_pair[:, RET_DK:].T)
    zeros = jnp.zeros((RET_DK, CHUNK), bf16)
    rhs = jnp.concatenate([jnp.concatenate([k_ts[0].astype(bf16), zeros], axis=1),
                           jnp.concatenate([zeros, k_ts[1].astype(bf16)], axis=1)], axis=0)
    return _dot(q_pair.astype(bf16), rhs), k_ts


def _retention_mix(sc, q, v, st, decay, xi):
    bf16 = jnp.bfloat16
    lhs = jnp.concatenate([(sc * decay).astype(bf16), (q * xi).astype(bf16)], axis=1)
    rhs = jnp.concatenate([v.astype(bf16), st.astype(bf16)], axis=0)
    return _dot(lhs, rhs)


def _retention_pair_kv(k_ts, v_pair, zeta_a, zeta_b):
    bf16 = jnp.bfloat16
    lhs = jnp.concatenate([(k_ts[0] * zeta_a).astype(bf16), (k_ts[1] * zeta_b).astype(bf16)], axis=1)
    vb = v_pair.astype(bf16)
    zeros = jnp.zeros((CHUNK, RET_DV), bf16)
    rhs = jnp.concatenate([jnp.concatenate([vb[:, :RET_DV], zeros], axis=1),
                           jnp.concatenate([zeros, vb[:, RET_DV:]], axis=1)], axis=0)
    return _dot(lhs, rhs)


def _retention_norm_gate(o, gate):
    mu = jnp.mean(o, axis=-1, keepdims=True)
    d = o - mu
    var = jnp.mean(d * d, axis=-1, keepdims=True)
    return d * lax.rsqrt(var + EPS) * _silu(gate)


def _attention_prep(k_both, v_both):
    bf16 = jnp.bfloat16
    return (k_both * (ATT_HEAD_DIM ** -0.5 * LOG2E)).astype(bf16), v_both.T.astype(bf16)


def _attention_scores(g, q_g, cur, prev):
    bf16 = jnp.bfloat16
    k_win = jnp.concatenate([cur[0], prev[0]], axis=0)
    zeros = jnp.zeros((ATT_HEAD_DIM, CHUNK), bf16)
    cols = []
    for p in range(2):
        q_t = q_g[:, p * PAIR:(p + 1) * PAIR].T.astype(bf16)
        for e in range(2):
            q_h = q_t[e * ATT_HEAD_DIM:(e + 1) * ATT_HEAD_DIM]
            cols.append(jnp.concatenate([q_h, zeros] if g == 0 else [zeros, q_h], axis=0))
    return _dot(k_win, jnp.concatenate(cols, axis=1))


def _attention_probs(g, s_t, bias_t_ref, first_idx, sinks_ref, tri_t):
    bf16 = jnp.bfloat16
    parts, inv_dens = [], []
    for hh in range(ATT_GROUP):
        hq = g * ATT_GROUP + hh
        s = s_t[:, hh * CHUNK:(hh + 1) * CHUNK]
        sf = jnp.where(tri_t, s[:CHUNK], s[CHUNK:]) - bias_t_ref[first_idx, hq]
        m = jnp.max(sf, axis=0, keepdims=True)
        ex = jnp.exp2(sf - m)
        den = jnp.sum(ex, axis=0, keepdims=True) + jnp.exp2(sinks_ref[hq] * LOG2E - m)
        inv_dens.append(1.0 / den)
        parts.append(jnp.concatenate([jnp.where(tri_t, ex, 0.0).astype(bf16),
                                      jnp.where(tri_t, 0.0, ex).astype(bf16)], axis=0))
    return jnp.concatenate(parts, axis=1), inv_dens


def _attention_pv(g, probs_t, cur, prev):
    out = _dot(jnp.concatenate([cur[1], prev[1]], axis=1), probs_t)
    return out[g * ATT_HEAD_DIM:(g + 1) * ATT_HEAD_DIM]


def _layer_kernel(sinks_ref, cdecay_ref, xnext_ref, xp_ref, gpre_ref, win_hbm, wout_hbm,
                  gpost_ref, decay_ref, xi_ref, zeta_t_ref, bias_ref,
                  o_ref, proj_a, proj_b, h_a, h_b, state_ref, carry_ref, mixed_ref,
                  win_ref, wout_ref, stage_ref, stage_sems, *, tm, tiles_per_seq, n_tiles):
    f32, bf16 = jnp.float32, jnp.bfloat16
    n = pl.program_id(0)
    first, last, even = n == 0, n == n_tiles, n % 2 == 0
    seq_start = (jnp.maximum(n - 1, 0) % tiles_per_seq) == 0

    def normalise(x_ref, h_ref):
        x = x_ref[...]
        ms = jnp.mean(x * x, axis=-1, keepdims=True)
        h_ref[...] = (x * lax.rsqrt(ms + EPS) * gpre_ref[...]).astype(bf16)

    @pl.when(seq_start)
    def _():
        state_ref[...] = jnp.zeros_like(state_ref)
        carry_ref[...] = jnp.zeros_like(carry_ref)

    row = lax.broadcasted_iota(jnp.int32, (CHUNK, CHUNK), 0)
    lane = lax.broadcasted_iota(jnp.int32, (CHUNK, CHUNK), 1)
    tri_t = row <= lane

    def project(h_ref, proj_w, row_parts=1):
        rp = tm // row_parts
        for j in range(0, IN_COLS, PROJ_COLS):
            for r in range(0, tm, rp):
                proj_w[r:r + rp, j:j + PROJ_COLS] = _dot(h_ref[r:r + rp, :], win_ref[:, j:j + PROJ_COLS])
            yield

    def mix(proj_r):
        prev = tuple(carry_ref[i] for i in range(N_CARRY))
        pairs, groups = range(RET_HEADS // 2), range(ATT_KV_HEADS)
        rp = tm // OUT_ROW_PARTS
        for c in range(tm // CHUNK):
            rows = slice(c * CHUNK, (c + 1) * CHUNK)

            def cols(base, i, width=CHUNK):
                return proj_r[rows, base + i * width:base + (i + 1) * width]

            scored = [_retention_pair_scores(cols(COL_RQ, hp, 2 * RET_DK), cols(COL_RK, hp, 2 * RET_DK))
                      for hp in pairs]
            cur = _attention_prep(cols(COL_AK, 0), cols(COL_AV, 0))
            s_t = [_attention_scores(g, cols(COL_AQ, g, 2 * PAIR), cur, prev) for g in groups]
            yield "scores"
            ret = []
            for hd in range(RET_HEADS):
                sc = scored[hd // 2][0][:, (hd % 2) * CHUNK:(hd % 2 + 1) * CHUNK]
                ret.append(_retention_mix(sc, cols(COL_RQ, hd), cols(COL_RV, hd), state_ref[hd],
                                          decay_ref[hd], xi_ref[hd]))
            for hp in pairs:
                kv = _retention_pair_kv(scored[hp][1], cols(COL_RV, hp, 2 * RET_DV),
                                        zeta_t_ref[2 * hp], zeta_t_ref[2 * hp + 1])
                for e in range(2):
                    hd = 2 * hp + e
                    state_ref[hd] = cdecay_ref[hd] * state_ref[hd] + kv[:, e * RET_DV:(e + 1) * RET_DV]
            yield "mix"
            for hd in range(RET_HEADS):
                mixed_ref[rows, hd * RET_DV:(hd + 1) * RET_DV] = _retention_norm_gate(
                    ret[hd], cols(COL_RG, hd)).astype(bf16)
            first_idx = seq_start.astype(jnp.int32) if c == 0 else 0
            probs = [_attention_probs(g, s_t[g], bias_ref, first_idx, sinks_ref, tri_t)
                     for g in groups]
            yield "softmax"
            pv = [_attention_pv(g, probs[g][0], cur, prev) for g in groups]
            yield "pv"
            for g in groups:
                for p in range(2):
                    i = g * 2 + p
                    heads = (2 * p, 2 * p + 1)
                    pair_t = jnp.concatenate(
                        [pv[g][:, hh * CHUNK:(hh + 1) * CHUNK] * probs[g][1][hh] for hh in heads], axis=0)
                    att = pair_t.T
                    mixed_ref[rows, RET_WIDTH + i * PAIR:RET_WIDTH + (i + 1) * PAIR] = (
                        att * _silu(cols(COL_AG, i))).astype(bf16)
            prev = cur
            yield "gate"
            if (c + 1) * CHUNK % rp == 0:
                r = (c + 1) * CHUNK - rp
                out = _dot(mixed_ref[r:r + rp, :], wout_ref[...])
                ms2 = jnp.mean(out * out, axis=-1, keepdims=True)
                o_ref[r:r + rp, :] = xp_ref[r:r + rp, :] + out * lax.rsqrt(ms2 + EPS) * gpost_ref[...]
                yield "out"
        for i in range(N_CARRY):
            carry_ref[i] = prev[i]

    def step(h_cur, h_next, proj_w, proj_r):
        pieces = project(h_cur, proj_w)
        for _ in range(PIECES_AFTER["start"]):
            next(pieces, None)
        for i, phase in enumerate(mix(proj_r)):
            for _ in range(PIECES_AFTER.get(phase, 0)):
                next(pieces, None)
            if i == NORMALISE_AFTER_PHASE:
                normalise(xnext_ref, h_next)
        for _ in pieces:
            pass

    w_blocks = ([(win_hbm, win_ref, j, min(STAGE_COLS, IN_COLS - j)) for j in range(0, IN_COLS, STAGE_COLS)]
                + [(wout_hbm, wout_ref, j, STAGE_COLS) for j in range(0, D_MODEL, STAGE_COLS)])

    def weight_copy(k):
        src, _, j, w = w_blocks[k]
        slot = k % N_STAGE
        return pltpu.make_async_copy(src.at[:, pl.ds(j, w)], stage_ref.at[slot, :, pl.ds(0, w)],
                                     stage_sems.at[slot])

    n_win_blocks = sum(1 for blk in w_blocks if blk[1] is win_ref)
    assert len(w_blocks) - n_win_blocks <= N_STAGE

    def land(k):
        _, dst, j, w = w_blocks[k]
        weight_copy(k).wait()
        dst[:, j:j + w] = stage_ref[k % N_STAGE, :, 0:w].astype(bf16)

    def first_step():
        for k in range(N_STAGE):
            weight_copy(k).start()
        normalise(xp_ref, h_a)
        pieces = project(h_a, proj_a, row_parts=2)
        for k in range(n_win_blocks):
            land(k)
            if k + N_STAGE < len(w_blocks):
                weight_copy(k + N_STAGE).start()
            if k >= 1:
                for _ in range(w_blocks[k - 1][3] // PROJ_COLS):
                    next(pieces, None)
        for _ in pieces:
            pass
        normalise(xnext_ref, h_b)

    @pl.when(n == 1)
    def _():
        for k in range(n_win_blocks, len(w_blocks)):
            land(k)

    pl.when(first)(first_step)

    @pl.when(last)
    def _():
        for _ in mix(proj_b if n_tiles % 2 == 0 else proj_a):
            pass

    @pl.when(even & jnp.logical_not(first | last))
    def _():
        step(h_a, h_b, proj_a, proj_b)

    @pl.when(jnp.logical_not(even | last))
    def _():
        step(h_b, h_a, proj_b, proj_a)


def _retention_tables():
    f32 = np.float32
    c = CHUNK
    log_gamma = np.log1p(-np.exp2(-5.0 - np.arange(RET_HEADS, dtype=f32))).astype(f32)
    pos = np.arange(c, dtype=f32)
    diff = pos[:, None] - pos[None, :]
    intra = np.where(diff >= 0, np.exp(log_gamma[:, None, None] * np.maximum(diff, f32(0))), f32(0))
    decay = (intra * f32(RET_DK ** -0.5)).astype(f32)
    xi = np.exp(log_gamma[:, None] * (pos + f32(1)))
    zeta = np.exp(log_gamma[:, None] * (f32(c - 1) - pos)) * f32(RET_DK ** -0.5)
    xi_b = np.broadcast_to(xi[:, :, None], (RET_HEADS, c, RET_DK)).astype(f32)
    zeta_t = np.broadcast_to(zeta[:, None, :], (RET_HEADS, RET_DK, c)).astype(f32)
    chunk_decay = np.exp(log_gamma * f32(c)).astype(f32)
    return decay, xi_b, zeta_t, chunk_decay


def _alibi_bias():
    f32 = np.float32
    t = CHUNK
    i = np.arange(t)[:, None]
    j = np.arange(t)[None, :]
    dist = np.where(j <= i, i - j, i + t - j).astype(f32)
    slopes = np.exp2(-8.0 * (np.arange(ATT_HEADS, dtype=f32) + f32(1)) / f32(ATT_HEADS)).astype(f32)
    bias = (f32(LOG2E) * slopes[:, None, None] * dist[None]).astype(f32)
    first = np.where((j <= i)[None], bias, f32(BIG)).astype(f32)
    return np.ascontiguousarray(np.swapaxes(np.stack([bias, first]), -1, -2))


def kernel(x, g_pre, w_in, sinks, w_out, g_post):
    bsz, seq, d_model = x.shape
    assert d_model == D_MODEL and w_in.shape == (D_MODEL, IN_COLS)
    assert w_out.shape == (D_MIX, D_MODEL) and D_MIX == D_MODEL and seq % TM == 0
    f32, bf16 = jnp.float32, jnp.bfloat16
    decay, xi_b, zeta_t, chunk_decay = _retention_tables()
    bias = _alibi_bias()
    n_tiles = bsz * seq // TM
    x2 = x.reshape(bsz * seq, D_MODEL)

    def const(shape):
        return pl.BlockSpec(shape, lambda n: (0,) * len(shape))

    smem = pl.BlockSpec(memory_space=pltpu.SMEM)
    out = pl.pallas_call(
        functools.partial(_layer_kernel, tm=TM, tiles_per_seq=seq // TM, n_tiles=n_tiles),
        grid=(n_tiles + 1,),
        in_specs=[
            smem,
            smem,
            pl.BlockSpec((TM, D_MODEL), lambda n: (jnp.minimum(n + 1, n_tiles - 1), 0)),
            pl.BlockSpec((TM, D_MODEL), lambda n: (jnp.maximum(n - 1, 0), 0)),
            const((1, D_MODEL)),
            pl.BlockSpec(memory_space=pl.ANY),
            pl.BlockSpec(memory_space=pl.ANY),
            const((1, D_MODEL)),
            const((RET_HEADS, CHUNK, CHUNK)),
            const((RET_HEADS, CHUNK, RET_DK)),
            const((RET_HEADS, RET_DK, CHUNK)),
            const((2, ATT_HEADS, CHUNK, CHUNK)),
        ],
        out_specs=pl.BlockSpec((TM, D_MODEL), lambda n: (jnp.maximum(n - 1, 0), 0)),
        out_shape=jax.ShapeDtypeStruct(x2.shape, x.dtype),
        scratch_shapes=[
            pltpu.VMEM((TM, IN_COLS), f32),
            pltpu.VMEM((TM, IN_COLS), f32),
            pltpu.VMEM((TM, D_MODEL), bf16),
            pltpu.VMEM((TM, D_MODEL), bf16),
            pltpu.VMEM((RET_HEADS, RET_DK, RET_DV), f32),
            pltpu.VMEM((N_CARRY, CHUNK, CHUNK), bf16),
            pltpu.VMEM((TM, D_MIX), bf16),
            pltpu.VMEM((D_MODEL, IN_COLS), bf16),
            pltpu.VMEM((D_MIX, D_MODEL), bf16),
            pltpu.VMEM((N_STAGE, D_MODEL, STAGE_COLS), f32),
            pltpu.SemaphoreType.DMA((N_STAGE,)),
        ],
        compiler_params=pltpu.CompilerParams(
            dimension_semantics=("arbitrary",),
            vmem_limit_bytes=VMEM_LIMIT_BYTES),
        name="hybrid_layer",
    )(sinks.astype(f32), chunk_decay, x2, x2, g_pre.reshape(1, D_MODEL).astype(f32),
      w_in.astype(f32), w_out.astype(f32), g_post.reshape(1, D_MODEL).astype(f32),
      decay, xi_b, zeta_t, bias)
    return out.reshape(x.shape)
```

```python
import functools

import jax
import jax.numpy as jnp
import numpy as np
from jax import lax
from jax.experimental import pallas as pl
from jax.experimental.pallas import tpu as pltpu

D_MODEL = 1024
RET_HEADS = 4
RET_DK = 128
RET_DV = 128
RET_WIDTH = RET_HEADS * RET_DV
CHUNK = 128
ATT_HEADS = 8
ATT_KV_HEADS = 2
ATT_GROUP = ATT_HEADS // ATT_KV_HEADS
ATT_HEAD_DIM = 64
ATT_WIDTH = ATT_HEADS * ATT_HEAD_DIM
ATT_KV_WIDTH = ATT_KV_HEADS * ATT_HEAD_DIM
D_MIX = RET_WIDTH + ATT_WIDTH
EPS = 1e-6
BIG = 1e30
LOG2E = 1.4426950408889634

COL_RQ = 0
COL_RK = COL_RQ + RET_HEADS * RET_DK
COL_RV = COL_RK + RET_HEADS * RET_DK
COL_RG = COL_RV + RET_WIDTH
COL_AQ = COL_RG + RET_WIDTH
COL_AK = COL_AQ + ATT_WIDTH
COL_AV = COL_AK + ATT_KV_WIDTH
COL_AG = COL_AV + ATT_KV_WIDTH
IN_COLS = COL_AG + ATT_WIDTH

PAIR = 2 * ATT_HEAD_DIM
assert PAIR == CHUNK and ATT_KV_WIDTH == CHUNK and RET_DK == CHUNK and RET_DV == CHUNK

TM = 512
VMEM_LIMIT_BYTES = 56 * 1024 * 1024
N_CARRY = 2
PROJ_COLS = 256
STAGE_COLS = 512
N_STAGE = 4
OUT_ROW_PARTS = 4
PIECES_AFTER = {"start": 1, "softmax": 1, "gate": 1, "out": 1}
NORMALISE_AFTER_PHASE = 4
assert IN_COLS % PROJ_COLS == 0


def _silu(x):
    hx = 0.5 * x
    return hx + hx * jnp.tanh(hx)


_dot = functools.partial(jnp.dot, preferred_element_type=jnp.float32)


def _retention_pair_scores(q_pair, k_pair):
    bf16 = jnp.bfloat16
    k_ts = (k_pair[:, :RET_DK].T, k_pair[:, RET_DK:].T)
    zeros = jnp.zeros((RET_DK, CHUNK), bf16)
    rhs = jnp.concatenate([jnp.concatenate([k_ts[0].astype(bf16), zeros], axis=1),
                           jnp.concatenate([zeros, k_ts[1].astype(bf16)], axis=1)], axis=0)
    return _dot(q_pair.astype(bf16), rhs), k_ts


def _retention_mix(sc, q, v, st, decay, xi):
    bf16 = jnp.bfloat16
    lhs = jnp.concatenate([(sc * decay).astype(bf16), (q * xi).astype(bf16)], axis=1)
    rhs = jnp.concatenate([v.astype(bf16), st.astype(bf16)], axis=0)
    return _dot(lhs, rhs)


def _retention_pair_kv(k_ts, v_pair, zeta_a, zeta_b):
    bf16 = jnp.bfloat16
    lhs = jnp.concatenate([(k_ts[0] * zeta_a).astype(bf16), (k_ts[1] * zeta_b).astype(bf16)], axis=1)
    vb = v_pair.astype(bf16)
    zeros = jnp.zeros((CHUNK, RET_DV), bf16)
    rhs = jnp.concatenate([jnp.concatenate([vb[:, :RET_DV], zeros], axis=1),
                           jnp.concatenate([zeros, vb[:, RET_DV:]], axis=1)], axis=0)
    return _dot(lhs, rhs)


def _retention_norm_gate(o, gate):
    mu = jnp.mean(o, axis=-1, keepdims=True)
    d = o - mu
    var = jnp.mean(d * d, axis=-1, keepdims=True)
    return d * lax.rsqrt(var + EPS) * _silu(gate)


def _attention_prep(k_both, v_both):
    bf16 = jnp.bfloat16
    return (k_both * (ATT_HEAD_DIM ** -0.5 * LOG2E)).astype(bf16), v_both.T.astype(bf16)


def _attention_scores(g, q_g, cur, prev):
    bf16 = jnp.bfloat16
    k_win = jnp.concatenate([cur[0], prev[0]], axis=0)
    zeros = jnp.zeros((ATT_HEAD_DIM, CHUNK), bf16)
    cols = []
    for p in range(2):
        q_t = q_g[:, p * PAIR:(p + 1) * PAIR].T.astype(bf16)
        for e in range(2):
            q_h = q_t[e * ATT_HEAD_DIM:(e + 1) * ATT_HEAD_DIM]
            cols.append(jnp.concatenate([q_h, zeros] if g == 0 else [zeros, q_h], axis=0))
    return _dot(k_win, jnp.concatenate(cols, axis=1))


def _attention_probs(g, s_t, bias_t_ref, first_idx, sinks_ref, tri_t):
    bf16 = jnp.bfloat16
    parts, inv_dens = [], []
    for hh in range(ATT_GROUP):
        hq = g * ATT_GROUP + hh
        s = s_t[:, hh * CHUNK:(hh + 1) * CHUNK]
        sf = jnp.where(tri_t, s[:CHUNK], s[CHUNK:]) - bias_t_ref[first_idx, hq]
        m = jnp.max(sf, axis=0, keepdims=True)
        ex = jnp.exp2(sf - m)
        den = jnp.sum(ex, axis=0, keepdims=True) + jnp.exp2(sinks_ref[hq] * LOG2E - m)
        inv_dens.append(1.0 / den)
        parts.append(jnp.concatenate([jnp.where(tri_t, ex, 0.0).astype(bf16),
                                      jnp.where(tri_t, 0.0, ex).astype(bf16)], axis=0))
    return jnp.concatenate(parts, axis=1), inv_dens


def _attention_pv(g, probs_t, cur, prev):
    out = _dot(jnp.concatenate([cur[1], prev[1]], axis=1), probs_t)
    return out[g * ATT_HEAD_DIM:(g + 1) * ATT_HEAD_DIM]


def _layer_kernel(sinks_ref, cdecay_ref, xnext_ref, xp_ref, gpre_ref, win_hbm, wout_hbm,
                  gpost_ref, decay_ref, xi_ref, zeta_t_ref, bias_ref,
                  o_ref, proj_a, proj_b, h_a, h_b, state_ref, carry_ref, mixed_ref,
                  win_ref, wout_ref, stage_ref, stage_sems, *, tm, tiles_per_seq, n_tiles):
    f32, bf16 = jnp.float32, jnp.bfloat16
    n = pl.program_id(0)
    first, last, even = n == 0, n == n_tiles, n % 2 == 0
    seq_start = (jnp.maximum(n - 1, 0) % tiles_per_seq) == 0

    def normalise(x_ref, h_ref):
        x = x_ref[...]
        ms = jnp.mean(x * x, axis=-1, keepdims=True)
        h_ref[...] = (x * lax.rsqrt(ms + EPS) * gpre_ref[...]).astype(bf16)

    @pl.when(seq_start)
    def _():
        state_ref[...] = jnp.zeros_like(state_ref)
        carry_ref[...] = jnp.zeros_like(carry_ref)

    row = lax.broadcasted_iota(jnp.int32, (CHUNK, CHUNK), 0)
    lane = lax.broadcasted_iota(jnp.int32, (CHUNK, CHUNK), 1)
    tri_t = row <= lane

    def project(h_ref, proj_w, row_parts=1):
        rp = tm // row_parts
        for j in range(0, IN_COLS, PROJ_COLS):
            for r in range(0, tm, rp):
                proj_w[r:r + rp, j:j + PROJ_COLS] = _dot(h_ref[r:r + rp, :], win_ref[:, j:j + PROJ_COLS])
            yield

    def mix(proj_r):
        prev = tuple(carry_ref[i] for i in range(N_CARRY))
        pairs, groups = range(RET_HEADS // 2), range(ATT_KV_HEADS)
        rp = tm // OUT_ROW_PARTS
        for c in range(tm // CHUNK):
            rows = slice(c * CHUNK, (c + 1) * CHUNK)

            def cols(base, i, width=CHUNK):
                return proj_r[rows, base + i * width:base + (i + 1) * width]

            scored = [_retention_pair_scores(cols(COL_RQ, hp, 2 * RET_DK), cols(COL_RK, hp, 2 * RET_DK))
                      for hp in pairs]
            cur = _attention_prep(cols(COL_AK, 0), cols(COL_AV, 0))
            s_t = [_attention_scores(g, cols(COL_AQ, g, 2 * PAIR), cur, prev) for g in groups]
            yield "scores"
            ret = []
            for hd in range(RET_HEADS):
                sc = scored[hd // 2][0][:, (hd % 2) * CHUNK:(hd % 2 + 1) * CHUNK]
                ret.append(_retention_mix(sc, cols(COL_RQ, hd), cols(COL_RV, hd), state_ref[hd],
                                          decay_ref[hd], xi_ref[hd]))
            for hp in pairs:
                kv = _retention_pair_kv(scored[hp][1], cols(COL_RV, hp, 2 * RET_DV),
                                        zeta_t_ref[2 * hp], zeta_t_ref[2 * hp + 1])
                for e in range(2):
                    hd = 2 * hp + e
                    state_ref[hd] = cdecay_ref[hd] * state_ref[hd] + kv[:, e * RET_DV:(e + 1) * RET_DV]
            yield "mix"
            for hd in range(RET_HEADS):
                mixed_ref[rows, hd * RET_DV:(hd + 1) * RET_DV] = _retention_norm_gate(
                    ret[hd], cols(COL_RG, hd)).astype(bf16)
            first_idx = seq_start.astype(jnp.int32) if c == 0 else 0
            probs = [_attention_probs(g, s_t[g], bias_ref, first_idx, sinks_ref, tri_t)
                     for g in groups]
            yield "softmax"
            pv = [_attention_pv(g, probs[g][0], cur, prev) for g in groups]
            yield "pv"
            for g in groups:
                for p in range(2):
                    i = g * 2 + p
                    heads = (2 * p, 2 * p + 1)
                    pair_t = jnp.concatenate(
                        [pv[g][:, hh * CHUNK:(hh + 1) * CHUNK] * probs[g][1][hh] for hh in heads], axis=0)
                    att = pair_t.T
                    mixed_ref[rows, RET_WIDTH + i * PAIR:RET_WIDTH + (i + 1) * PAIR] = (
                        att * _silu(cols(COL_AG, i))).astype(bf16)
            prev = cur
            yield "gate"
            if (c + 1) * CHUNK % rp == 0:
                r = (c + 1) * CHUNK - rp
                out = _dot(mixed_ref[r:r + rp, :], wout_ref[...])
                ms2 = jnp.mean(out * out, axis=-1, keepdims=True)
                o_ref[r:r + rp, :] = xp_ref[r:r + rp, :] + out * lax.rsqrt(ms2 + EPS) * gpost_ref[...]
                yield "out"
        for i in range(N_CARRY):
            carry_ref[i] = prev[i]

    def step(h_cur, h_next, proj_w, proj_r):
        pieces = project(h_cur, proj_w)
        for _ in range(PIECES_AFTER["start"]):
            next(pieces, None)
        for i, phase in enumerate(mix(proj_r)):
            for _ in range(PIECES_AFTER.get(phase, 0)):
                next(pieces, None)
            if i == NORMALISE_AFTER_PHASE:
                normalise(xnext_ref, h_next)
        for _ in pieces:
            pass

    w_blocks = ([(win_hbm, win_ref, j, min(STAGE_COLS, IN_COLS - j)) for j in range(0, IN_COLS, STAGE_COLS)]
                + [(wout_hbm, wout_ref, j, STAGE_COLS) for j in range(0, D_MODEL, STAGE_COLS)])

    def weight_copy(k):
        src, _, j, w = w_blocks[k]
        slot = k % N_STAGE
        return pltpu.make_async_copy(src.at[:, pl.ds(j, w)], stage_ref.at[slot, :, pl.ds(0, w)],
                                     stage_sems.at[slot])

    n_win_blocks = sum(1 for blk in w_blocks if blk[1] is win_ref)
    assert len(w_blocks) - n_win_blocks <= N_STAGE

    def land(k):
        _, dst, j, w = w_blocks[k]
        weight_copy(k).wait()
        dst[:, j:j + w] = stage_ref[k % N_STAGE, :, 0:w].astype(bf16)

    def first_step():
        for k in range(N_STAGE):
            weight_copy(k).start()
        normalise(xp_ref, h_a)
        pieces = project(h_a, proj_a, row_parts=2)
        for k in range(n_win_blocks):
            land(k)
            if k + N_STAGE < len(w_blocks):
                weight_copy(k + N_STAGE).start()
            if k >= 1:
                for _ in range(w_blocks[k - 1][3] // PROJ_COLS):
                    next(pieces, None)
        for _ in pieces:
            pass
        normalise(xnext_ref, h_b)

    @pl.when(n == 1)
    def _():
        for k in range(n_win_blocks, len(w_blocks)):
            land(k)

    pl.when(first)(first_step)

    @pl.when(last)
    def _():
        for _ in mix(proj_b if n_tiles % 2 == 0 else proj_a):
            pass

    @pl.when(even & jnp.logical_not(first | last))
    def _():
        step(h_a, h_b, proj_a, proj_b)

    @pl.when(jnp.logical_not(even | last))
    def _():
        step(h_b, h_a, proj_b, proj_a)


def _retention_tables():
    f32 = np.float32
    c = CHUNK
    log_gamma = np.log1p(-np.exp2(-5.0 - np.arange(RET_HEADS, dtype=f32))).astype(f32)
    pos = np.arange(c, dtype=f32)
    diff = pos[:, None] - pos[None, :]
    intra = np.where(diff >= 0, np.exp(log_gamma[:, None, None] * np.maximum(diff, f32(0))), f32(0))
    decay = (intra * f32(RET_DK ** -0.5)).astype(f32)
    xi = np.exp(log_gamma[:, None] * (pos + f32(1)))
    zeta = np.exp(log_gamma[:, None] * (f32(c - 1) - pos)) * f32(RET_DK ** -0.5)
    xi_b = np.broadcast_to(xi[:, :, None], (RET_HEADS, c, RET_DK)).astype(f32)
    zeta_t = np.broadcast_to(zeta[:, None, :], (RET_HEADS, RET_DK, c)).astype(f32)
    chunk_decay = np.exp(log_gamma * f32(c)).astype(f32)
    return decay, xi_b, zeta_t, chunk_decay


def _alibi_bias():
    f32 = np.float32
    t = CHUNK
    i = np.arange(t)[:, None]
    j = np.arange(t)[None, :]
    dist = np.where(j <= i, i - j, i + t - j).astype(f32)
    slopes = np.exp2(-8.0 * (np.arange(ATT_HEADS, dtype=f32) + f32(1)) / f32(ATT_HEADS)).astype(f32)
    bias = (f32(LOG2E) * slopes[:, None, None] * dist[None]).astype(f32)
    first = np.where((j <= i)[None], bias, f32(BIG)).astype(f32)
    return np.ascontiguousarray(np.swapaxes(np.stack([bias, first]), -1, -2))


def kernel(x, g_pre, w_in, sinks, w_out, g_post):
    bsz, seq, d_model = x.shape
    assert d_model == D_MODEL and w_in.shape == (D_MODEL, IN_COLS)
    assert w_out.shape == (D_MIX, D_MODEL) and D_MIX == D_MODEL and seq % TM == 0
    f32, bf16 = jnp.float32, jnp.bfloat16
    decay, xi_b, zeta_t, chunk_decay = _retention_tables()
    bias = _alibi_bias()
    n_tiles = bsz * seq // TM
    x2 = x.reshape(bsz * seq, D_MODEL)

    def const(shape):
        return pl.BlockSpec(shape, lambda n: (0,) * len(shape))

    smem = pl.BlockSpec(memory_space=pltpu.SMEM)
    out = pl.pallas_call(
        functools.partial(_layer_kernel, tm=TM, tiles_per_seq=seq // TM, n_tiles=n_tiles),
        grid=(n_tiles + 1,),
        in_specs=[
            smem,
            smem,
            pl.BlockSpec((TM, D_MODEL), lambda n: (jnp.minimum(n + 1, n_tiles - 1), 0)),
            pl.BlockSpec((TM, D_MODEL), lambda n: (jnp.maximum(n - 1, 0), 0)),
            const((1, D_MODEL)),
            pl.BlockSpec(memory_space=pl.ANY),
            pl.BlockSpec(memory_space=pl.ANY),
            const((1, D_MODEL)),
            const((RET_HEADS, CHUNK, CHUNK)),
            const((RET_HEADS, CHUNK, RET_DK)),
            const((RET_HEADS, RET_DK, CHUNK)),
            const((2, ATT_HEADS, CHUNK, CHUNK)),
        ],
        out_specs=pl.BlockSpec((TM, D_MODEL), lambda n: (jnp.maximum(n - 1, 0), 0)),
        out_shape=jax.ShapeDtypeStruct(x2.shape, x.dtype),
        scratch_shapes=[
            pltpu.VMEM((TM, IN_COLS), f32),
            pltpu.VMEM((TM, IN_COLS), f32),
            pltpu.VMEM((TM, D_MODEL), bf16),
            pltpu.VMEM((TM, D_MODEL), bf16),
            pltpu.VMEM((RET_HEADS, RET_DK, RET_DV), f32),
            pltpu.VMEM((N_CARRY, CHUNK, CHUNK), bf16),
            pltpu.VMEM((TM, D_MIX), bf16),
            pltpu.VMEM((D_MODEL, IN_COLS), bf16),
            pltpu.VMEM((D_MIX, D_MODEL), bf16),
            pltpu.VMEM((N_STAGE, D_MODEL, STAGE_COLS), f32),
            pltpu.SemaphoreType.DMA((N_STAGE,)),
        ],
        compiler_params=pltpu.CompilerParams(
            dimension_semantics=("arbitrary",),
            vmem_limit_bytes=VMEM_LIMIT_BYTES),
        name="hybrid_layer",
    )(sinks.astype(f32), chunk_decay, x2, x2, g_pre.reshape(1, D_MODEL).astype(f32),
      w_in.astype(f32), w_out.astype(f32), g_post.reshape(1, D_MODEL).astype(f32),
      decay, xi_b, zeta_t, bias)
    return out.reshape(x.shape)
```
